```python
import math
import jax
import jax.numpy as jnp
from jax import lax
import numpy as np

D_MODEL = 1024
BATCH = 2
SEQ = 8192
DEPTH = 1
DEC_BATCH = 8
DEC_SEQ = 16
PAST_LEN = 2048

CHUNK = 64
N_HEADS_A = 8
N_KV_A = 2
GROUP_A = N_HEADS_A // N_KV_A
HEAD_DIM_A = 64
N_IDX_HEADS = 4
IDX_DIM = 64
TOPK_MAX = 256
Q_BLOCK = 128
N_HEADS_B = 4
DK_B = 128
DV_B = 128
CONV_W = 4
D_FF = 2816
EPS = 1e-6

WIDTH_A = N_HEADS_A * HEAD_DIM_A
WIDTH_KV_A = N_KV_A * HEAD_DIM_A
WIDTH_IDX_Q = N_IDX_HEADS * IDX_DIM
WIDTH_QK_B = N_HEADS_B * DK_B
WIDTH_V_B = N_HEADS_B * DV_B
CONV_CH = 2 * WIDTH_QK_B + WIDTH_V_B
SPLITS = (WIDTH_A, WIDTH_KV_A, WIDTH_KV_A, WIDTH_IDX_Q, IDX_DIM, N_IDX_HEADS,
          WIDTH_QK_B, WIDTH_QK_B, WIDTH_V_B, WIDTH_V_B, N_HEADS_B, N_HEADS_B,
          D_MODEL, D_MODEL)
D_IN = (WIDTH_A + 2 * WIDTH_KV_A + WIDTH_IDX_Q + IDX_DIM + N_IDX_HEADS
        + 2 * WIDTH_QK_B + 2 * WIDTH_V_B + 2 * N_HEADS_B + 2 * D_MODEL)

kernel_name = 'hybrid_dsa_gdn_macaron_stream_step'


def rms_norm(x, g):
    xf = x.astype(jnp.float32)
    y = xf * lax.rsqrt(jnp.mean(xf * xf, axis=-1, keepdims=True) + EPS)
    return (y * g.astype(jnp.float32)).astype(x.dtype)


def l2norm(x):
    return x * lax.rsqrt(jnp.sum(x * x, axis=-1, keepdims=True) + EPS)


def swiglu(x, w_gate, w_up, w_down):
    return (jax.nn.silu(x @ w_gate) * (x @ w_up)) @ w_down


def split_columns(p):
    offsets = np.cumsum(np.array(SPLITS))[:-1].tolist()
    return jnp.split(p, offsets, axis=-1)


def dsa_block(q, qi, wi, pos, k, v, ki, n_sel):
    f32 = jnp.float32
    n_keys = k.shape[1]
    limit = jnp.minimum((pos // CHUNK + 1) * CHUNK, n_keys)
    admissible = jnp.arange(n_keys)[None, :] < limit[:, None]
    logits = jnp.einsum('bthd,bsd->bths', qi.astype(f32), ki.astype(f32))
    score = jnp.einsum('bths,bth->bts', jax.nn.relu(logits), wi.astype(f32))
    score = jnp.where(admissible[None], score, -jnp.inf)
    _, sel = lax.top_k(score, n_sel)
    sel_ok = sel < limit[None, :, None]
    kv = jnp.concatenate([k, v], axis=-1)
    kv_sel = jax.vmap(lambda rows, idx: rows[idx])(kv, sel).astype(f32)
    k_sel = kv_sel[..., :HEAD_DIM_A]
    v_sel = kv_sel[..., HEAD_DIM_A:]
    s = jnp.einsum('btngd,btknd->btngk', q.astype(f32), k_sel) * (HEAD_DIM_A ** -0.5)
    s = jnp.where(sel_ok[:, :, None, None, :], s, -jnp.inf)
    p = jax.nn.softmax(s, axis=-1)
    o = jnp.einsum('btngk,btknd->btngd', p, v_sel)
    return o.astype(q.dtype)


def dsa_prompt(q, qi, wi, k, v, ki):
    b, t = q.shape[0], q.shape[1]
    nb = t // Q_BLOCK
    n_sel = min(TOPK_MAX, k.shape[1] // 4)

    def to_blocks(a):
        return jnp.moveaxis(a.reshape((b, nb, Q_BLOCK) + a.shape[2:]), 1, 0)

    pos = jnp.arange(t, dtype=jnp.int32).reshape(nb, Q_BLOCK)

    def one_block(args):
        qb, qib, wib, posb = args
        return dsa_block(qb, qib, wib, posb, k, v, ki, n_sel)

    o = lax.map(one_block, (to_blocks(q), to_blocks(qi), to_blocks(wi), pos))
    return jnp.moveaxis(o, 0, 1).reshape((b, t) + o.shape[3:])


def dsa_step(q, qi, wi, k_all, v_all, ki_all, past_len):
    t = q.shape[1]
    n_sel = min(TOPK_MAX, k_all.shape[1] // 4)
    pos = past_len + jnp.arange(t, dtype=jnp.int32)
    return dsa_block(q, qi, wi, pos, k_all, v_all, ki_all, n_sel)


def delta_chunk(state, inputs):
    q, k, v, g, beta = inputs
    c = q.shape[2]
    tri = jnp.tril(jnp.ones((c, c), dtype=bool))
    strict = jnp.tril(jnp.ones((c, c), dtype=bool), k=-1)
    gc = jnp.cumsum(g, axis=-1)
    decay = jnp.exp(jnp.where(tri, gc[..., :, None] - gc[..., None, :], -jnp.inf))
    k_beta = k * beta[..., None]
    lower = jnp.where(strict, jnp.einsum('bhid,bhjd->bhij', k_beta, k) * decay, 0.0)
    a = jnp.eye(c, dtype=q.dtype) + lower
    rhs = jnp.concatenate([v * beta[..., None], k_beta * jnp.exp(gc)[..., None]], axis=-1)
    sol = lax.linalg.triangular_solve(a, rhs, left_side=True, lower=True, unit_diagonal=True)
    u = sol[..., :DV_B]
    w = sol[..., DV_B:]
    v_new = u - jnp.einsum('bhcd,bhde->bhce', w, state)
    attn = jnp.where(tri, jnp.einsum('bhid,bhjd->bhij', q, k) * decay, 0.0)
    o = (jnp.einsum('bhcd,bhde->bhce', q * jnp.exp(gc)[..., None], state)
         + jnp.einsum('bhij,bhje->bhie', attn, v_new))
    g_last = gc[..., -1:]
    new_state = (state * jnp.exp(g_last)[..., None]
                 + jnp.einsum('bhcd,bhce->bhde', k * jnp.exp(g_last - gc)[..., None], v_new))
    return new_state, o


def gated_deltanet(qb, kb, vb, zb, bb, ab, conv_state, state, conv_w, a_log, dt_bias, norm_g, chunk):
    f32 = jnp.float32
    b, t, _ = qb.shape
    xc = jnp.concatenate([qb, kb, vb], axis=-1)
    xpad = jnp.concatenate([conv_state.astype(xc.dtype), xc], axis=1)
    yc = xpad[:, 0:t] * conv_w[0]
    for j in range(1, CONV_W):
        yc = yc + xpad[:, j:j + t] * conv_w[j]
    new_conv = xpad[:, t:]
    yc = jax.nn.silu(yc).astype(f32)
    q, k, v = jnp.split(yc, [WIDTH_QK_B, 2 * WIDTH_QK_B], axis=-1)

    def heads(a, d):
        return jnp.transpose(a.reshape(b, t, N_HEADS_B, d), (0, 2, 1, 3))

    q = l2norm(heads(q, DK_B)) * (DK_B ** -0.5)
    k = l2norm(heads(k, DK_B))
    v = heads(v, DV_B)
    beta = jnp.transpose(jax.nn.sigmoid(bb.astype(f32)), (0, 2, 1))
    g = -jnp.exp(a_log.astype(f32)) * jax.nn.softplus(ab.astype(f32) + dt_bias.astype(f32))
    g = jnp.transpose(g, (0, 2, 1))
    nc = t // chunk

    def to_chunks(a):
        return jnp.moveaxis(a.reshape((b, N_HEADS_B, nc, chunk) + a.shape[3:]), 2, 0)

    s_final, o = lax.scan(delta_chunk, state.astype(f32),
                          (to_chunks(q), to_chunks(k), to_chunks(v), to_chunks(g), to_chunks(beta)))
    o = jnp.moveaxis(o, 0, 2).reshape(b, N_HEADS_B, t, DV_B)
    o = jnp.transpose(o, (0, 2, 1, 3))
    z = zb.reshape(b, t, N_HEADS_B, DV_B).astype(f32)
    o = rms_norm(o, norm_g) * jax.nn.silu(z)
    return o.reshape(b, t, WIDTH_V_B).astype(qb.dtype), new_conv, s_final


def layer(x, w, past):
    b, t, _ = x.shape
    x = x + 0.5 * swiglu(rms_norm(x, w['ffn1_norm']), w['ffn1_w_gate'], w['ffn1_w_up'], w['ffn1_w_down'])
    h = rms_norm(x, w['mix_norm'])
    (qa, ka, va, qi, ki, wi, qb, kb, vb, zb, bb, ab, gate_a, gate_b) = split_columns(h @ w['w_in'])
    qa = qa.reshape(b, t, N_KV_A, GROUP_A, HEAD_DIM_A)
    ka = ka.reshape(b, t, N_KV_A, HEAD_DIM_A)
    va = va.reshape(b, t, N_KV_A, HEAD_DIM_A)
    qi = qi.reshape(b, t, N_IDX_HEADS, IDX_DIM) * (IDX_DIM ** -0.5)
    wi = wi * (N_IDX_HEADS ** -0.5)
    if past is None:
        oa = dsa_prompt(qa, qi, wi, ka, va, ki)
        conv0 = jnp.zeros((b, CONV_W - 1, CONV_CH), x.dtype)
        s0 = jnp.zeros((b, N_HEADS_B, DK_B, DV_B), jnp.float32)
        chunk = CHUNK
    else:
        ck, cv, cki, conv0, s0 = past
        oa = dsa_step(qa, qi, wi,
                      jnp.concatenate([ck, ka], axis=1),
                      jnp.concatenate([cv, va], axis=1),
                      jnp.concatenate([cki, ki], axis=1),
                      ck.shape[1])
        chunk = t
    ob, conv_new, s_new = gated_deltanet(qb, kb, vb, zb, bb, ab, conv0, s0, w['conv_w'],
                                         w['a_log'], w['dt_bias'], w['delta_norm'], chunk)
    ya = oa.reshape(b, t, WIDTH_A) @ w['w_branch_a']
    yb = ob @ w['w_branch_b']
    merged = jax.nn.sigmoid(gate_a) * ya + jax.nn.sigmoid(gate_b) * yb
    x = x + merged @ w['w_out']
    x = x + 0.5 * swiglu(rms_norm(x, w['ffn2_norm']), w['ffn2_w_gate'], w['ffn2_w_up'], w['ffn2_w_down'])
    return x, (ka, va, ki, conv_new, s_new.astype(x.dtype))


def setup_inputs(seed: int = 0) -> dict:
    key = jax.random.key(seed)
    ks = jax.random.split(key, 26)
    f32 = jnp.float32

    def normal(k, shape, scale=1.0):
        return scale * jax.random.normal(k, shape, f32)

    def gain(k, shape):
        return 1.0 + 0.02 * jax.random.normal(k, shape, f32)

    dt = jnp.exp(jax.random.uniform(ks[15], (DEPTH, N_HEADS_B), f32, math.log(1e-3), math.log(1e-1)))
    return {
        'x_prompt': normal(ks[0], (BATCH, SEQ, D_MODEL)),
        'x_sample': normal(ks[1], (DEC_BATCH, DEC_SEQ, D_MODEL)),
        'cache_attn_k': normal(ks[2], (DEPTH, DEC_BATCH, PAST_LEN, N_KV_A, HEAD_DIM_A)),
        'cache_attn_v': normal(ks[3], (DEPTH, DEC_BATCH, PAST_LEN, N_KV_A, HEAD_DIM_A)),
        'cache_idx_k': normal(ks[4], (DEPTH, DEC_BATCH, PAST_LEN, IDX_DIM)),
        'state_conv': normal(ks[5], (DEPTH, DEC_BATCH, CONV_W - 1, CONV_CH)),
        'state_delta': normal(ks[6], (DEPTH, DEC_BATCH, N_HEADS_B, DK_B, DV_B), 0.5),
        'ffn1_norm': gain(ks[7], (DEPTH, D_MODEL)),
        'ffn1_w_gate': normal(ks[8], (DEPTH, D_MODEL, D_FF), D_MODEL ** -0.5),
        'ffn1_w_up': normal(ks[9], (DEPTH, D_MODEL, D_FF), D_MODEL ** -0.5),
        'ffn1_w_down': normal(ks[10], (DEPTH, D_FF, D_MODEL), D_FF ** -0.5),
        'mix_norm': gain(ks[11], (DEPTH, D_MODEL)),
        'w_in': normal(ks[12], (DEPTH, D_MODEL, D_IN), D_MODEL ** -0.5),
        'conv_w': normal(ks[13], (DEPTH, CONV_W, CONV_CH), CONV_W ** -0.5),
        'a_log': jnp.log(jax.random.uniform(ks[14], (DEPTH, N_HEADS_B), f32, 1.0, 16.0)),
        'dt_bias': dt + jnp.log(-jnp.expm1(-dt)),
        'delta_norm': gain(ks[16], (DEPTH, DV_B)),
        'w_branch_a': normal(ks[17], (DEPTH, WIDTH_A, D_MODEL), WIDTH_A ** -0.5),
        'w_branch_b': normal(ks[18], (DEPTH, WIDTH_V_B, D_MODEL), WIDTH_V_B ** -0.5),
        'w_out': normal(ks[19], (DEPTH, D_MODEL, D_MODEL), D_MODEL ** -0.5),
        'ffn2_norm': gain(ks[20], (DEPTH, D_MODEL)),
        'ffn2_w_gate': normal(ks[21], (DEPTH, D_MODEL, D_FF), D_MODEL ** -0.5),
        'ffn2_w_up': normal(ks[22], (DEPTH, D_MODEL, D_FF), D_MODEL ** -0.5),
        'ffn2_w_down': normal(ks[23], (DEPTH, D_FF, D_MODEL), D_FF ** -0.5),
        'final_norm': gain(ks[24], (D_MODEL,)),
    }


def reference(x_prompt, x_sample, cache_attn_k, cache_attn_v, cache_idx_k, state_conv, state_delta,
              ffn1_norm, ffn1_w_gate, ffn1_w_up, ffn1_w_down, mix_norm, w_in, conv_w, a_log, dt_bias,
              delta_norm, w_branch_a, w_branch_b, w_out, ffn2_norm, ffn2_w_gate, ffn2_w_up, ffn2_w_down,
              final_norm):
    xp = x_prompt
    xs = x_sample
    st_p = []
    st_s = []
    for l in range(DEPTH):
        w = {
            'ffn1_norm': ffn1_norm[l], 'ffn1_w_gate': ffn1_w_gate[l], 'ffn1_w_up': ffn1_w_up[l],
            'ffn1_w_down': ffn1_w_down[l], 'mix_norm': mix_norm[l], 'w_in': w_in[l], 'conv_w': conv_w[l],
            'a_log': a_log[l], 'dt_bias': dt_bias[l], 'delta_norm': delta_norm[l],
            'w_branch_a': w_branch_a[l], 'w_branch_b': w_branch_b[l], 'w_out': w_out[l],
            'ffn2_norm': ffn2_norm[l], 'ffn2_w_gate': ffn2_w_gate[l], 'ffn2_w_up': ffn2_w_up[l],
            'ffn2_w_down': ffn2_w_down[l],
        }
        xp, sp = layer(xp, w, None)
        xs, ss = layer(xs, w, (cache_attn_k[l], cache_attn_v[l], cache_idx_k[l], state_conv[l], state_delta[l]))
        st_p.append(sp)
        st_s.append(ss)
    y_prompt = rms_norm(xp, final_norm)
    y_sample = rms_norm(xs, final_norm)

    def stacked(states, i):
        return jnp.stack([s[i] for s in states], axis=0)

    return (y_prompt, y_sample,
            stacked(st_p, 0), stacked(st_p, 1), stacked(st_p, 2), stacked(st_p, 3), stacked(st_p, 4),
            stacked(st_s, 0), stacked(st_s, 1), stacked(st_s, 2), stacked(st_s, 3), stacked(st_s, 4))
```

```python
import functools

import jax
import jax.numpy as jnp
from jax import lax
from jax.experimental import pallas as pl
from jax.experimental.pallas import tpu as pltpu

F32 = jnp.float32
BF16 = jnp.bfloat16
I32 = jnp.int32

CHUNK = 64
N_HEADS_A = 8
N_KV_A = 2
GROUP_A = N_HEADS_A // N_KV_A
HEAD_DIM_A = 64
N_IDX_HEADS = 4
IDX_DIM = 64
TOPK_MAX = 256
N_HEADS_B = 4
DK_B = 128
DV_B = 128
CONV_W = 4
EPS = 1e-6

WIDTH_A = N_HEADS_A * HEAD_DIM_A
WIDTH_KV_A = N_KV_A * HEAD_DIM_A
WIDTH_IDX_Q = N_IDX_HEADS * IDX_DIM
WIDTH_QK_B = N_HEADS_B * DK_B
WIDTH_V_B = N_HEADS_B * DV_B
CONV_CH = 2 * WIDTH_QK_B + WIDTH_V_B

MISC_W = 128
MISC_WI = IDX_DIM
MISC_BETA = MISC_WI + N_IDX_HEADS
MISC_A = 72

V7X_VMEM_LIMIT = 56 * 1024 * 1024
LANES = 128
INT_MIN = -2 ** 31
INT_MAX = 2 ** 31 - 1
NEG_BIG = -1e30

HIGHEST = lax.Precision.HIGHEST


def _sigmoid(x):
    return 1.0 / (1.0 + jnp.exp(-x))


def _softplus(x):
    return jnp.maximum(x, 0.0) + jnp.log(1.0 + jnp.exp(-jnp.abs(x)))


def _rms(x, g):
    return x * lax.rsqrt(jnp.mean(x * x, axis=-1, keepdims=True) + EPS) * g


def _dot(a, b, precision=None):
    return jnp.dot(a, b, preferred_element_type=F32, precision=precision)


def _dot_nt(a, b, precision=None):
    return lax.dot_general(a, b, (((1,), (1,)), ((), ())), preferred_element_type=F32, precision=precision)


def _dot_tn(a, b, precision=None):
    return lax.dot_general(a, b, (((0,), (0,)), ((), ())), preferred_element_type=F32, precision=precision)


def _resident(shape):
    nd = len(shape)
    return pl.BlockSpec(shape, lambda *_: (0,) * nd, pipeline_mode=pl.Buffered(1))


def _params(n_axes):
    return pltpu.CompilerParams(dimension_semantics=("arbitrary",) * n_axes, vmem_limit_bytes=V7X_VMEM_LIMIT)


def _swiglu_residual(x, g_ref, wg_ref, wu_ref, wd_ref, n_fc):
    hb = _rms(x, g_ref[...]).astype(BF16)
    fc = wg_ref.shape[1] // n_fc
    acc = None
    for c in range(n_fc):
        sl = slice(c * fc, (c + 1) * fc)
        gate = _dot(hb, wg_ref[:, sl])
        up = _dot(hb, wu_ref[:, sl])
        a = (gate * _sigmoid(gate) * up).astype(BF16)
        part = _dot(a, wd_ref[sl, :])
        acc = part if acc is None else acc + part
    return x + 0.5 * acc


def _ffn_kernel(x_ref, g_ref, wg_ref, wu_ref, wd_ref, o_ref, *, n_fc):
    o_ref[...] = _swiglu_residual(x_ref[...], g_ref, wg_ref, wu_ref, wd_ref, n_fc)


def _ffn_call(x, g, wg, wu, wd, tm, n_fc):
    n, d = x.shape
    f = wg.shape[1]
    return pl.pallas_call(
        functools.partial(_ffn_kernel, n_fc=n_fc),
        out_shape=jax.ShapeDtypeStruct((n, d), F32),
        grid=(n // tm,),
        in_specs=[pl.BlockSpec((tm, d), lambda i: (i, 0)), _resident((1, d)),
                  _resident((d, f)), _resident((d, f)), _resident((f, d))],
        out_specs=pl.BlockSpec((tm, d), lambda i: (i, 0)),
        compiler_params=_params(1),
        name="ffn1",
    )(x, g, wg, wu, wd)


def _out_kernel(x_ref, oa_ref, ob_ref, ga_ref, gb_ref, wa_ref, wb_ref, wo_ref,
                g_ref, wg_ref, wu_ref, wd_ref, gf_ref, o_ref, *, n_fc, final):
    ya = _dot(oa_ref[...], wa_ref[...])
    yb = _dot(ob_ref[...], wb_ref[...])
    merged = _sigmoid(ga_ref[...]) * ya + _sigmoid(gb_ref[...]) * yb
    x2 = x_ref[...] + _dot(merged.astype(BF16), wo_ref[...])
    x3 = _swiglu_residual(x2, g_ref, wg_ref, wu_ref, wd_ref, n_fc)
    o_ref[...] = _rms(x3, gf_ref[...]) if final else x3


def _out_call(x1, oa, ob, ga, gb, wa, wb, wo, g, wg, wu, wd, gf, final, tm, n_fc):
    n, d = x1.shape
    f = wg.shape[1]
    tok = lambda w: pl.BlockSpec((tm, w), lambda i: (i, 0))
    return pl.pallas_call(
        functools.partial(_out_kernel, n_fc=n_fc, final=final),
        out_shape=jax.ShapeDtypeStruct((n, d), F32),
        grid=(n // tm,),
        in_specs=[tok(d), tok(WIDTH_A), tok(WIDTH_V_B), tok(d), tok(d),
                  _resident((WIDTH_A, d)), _resident((WIDTH_V_B, d)), _resident((d, d)),
                  _resident((1, d)), _resident((d, f)), _resident((d, f)), _resident((f, d)),
                  _resident((1, d))],
        out_specs=tok(d),
        compiler_params=_params(1),
        name="merge_ffn2",
    )(x1, oa, ob, ga, gb, wa, wb, wo, g, wg, wu, wd, gf)


def _proj_kernel(x_ref, g_ref, wqa_ref, wk_ref, wv_ref, wqi_ref, wmisc_ref, wxc_ref, wz_ref, wga_ref, wgb_ref,
                 qa_ref, k_ref, v_ref, kb_ref, vb_ref, qi_ref, ki_ref, kib_ref, misc_ref, xc_ref, z_ref,
                 ga_ref, gb_ref):
    hb = _rms(x_ref[...], g_ref[...]).astype(BF16)
    qa_ref[...] = (_dot(hb, wqa_ref[...]) * (HEAD_DIM_A ** -0.5)).astype(BF16)
    k = _dot(hb, wk_ref[...])
    k_ref[...] = k
    kb_ref[...] = k.astype(BF16)
    v = _dot(hb, wv_ref[...])
    v_ref[...] = v
    vb_ref[...] = v.astype(BF16)
    qi_ref[...] = _dot(hb, wqi_ref[...])
    misc = _dot(hb, wmisc_ref[...])
    misc_ref[...] = misc
    ki = misc[:, :IDX_DIM]
    ki_ref[...] = ki
    kib_ref[...] = ki.astype(BF16)
    xc_ref[...] = _dot(hb, wxc_ref[...])
    z_ref[...] = _dot(hb, wz_ref[...])
    ga_ref[...] = _dot(hb, wga_ref[...])
    gb_ref[...] = _dot(hb, wgb_ref[...])


def _proj_call(x1, g, ws, tm):
    n, d = x1.shape
    widths = [(WIDTH_A, BF16), (WIDTH_KV_A, F32), (WIDTH_KV_A, F32), (WIDTH_KV_A, BF16), (WIDTH_KV_A, BF16),
              (WIDTH_IDX_Q, F32), (IDX_DIM, F32), (IDX_DIM, BF16), (MISC_W, F32), (CONV_CH, F32),
              (WIDTH_V_B, F32), (d, F32), (d, F32)]
    tok = lambda w: pl.BlockSpec((tm, w), lambda i: (i, 0))
    return pl.pallas_call(
        _proj_kernel,
        out_shape=[jax.ShapeDtypeStruct((n, w), dt) for w, dt in widths],
        grid=(n // tm,),
        in_specs=[tok(d), _resident((1, d))] + [_resident(w.shape) for w in ws],
        out_specs=[tok(w) for w, _ in widths],
        compiler_params=_params(1),
        name="in_proj",
    )(x1, g, *ws)


def _unit_lower_inverse(a, c):
    n = -a
    eye = (lax.broadcasted_iota(I32, (c, c), 0) == lax.broadcasted_iota(I32, (c, c), 1)).astype(F32)
    t = eye + n
    p = n
    span = 2
    while span < c:
        p = _dot(p, p, HIGHEST)
        t = t + _dot(t, p, HIGHEST)
        span *= 2
    return t


def _gdn_kernel(xc_ref, z_ref, misc_ref, conv0_ref, s0_ref, convw_ref, alog_ref, dtb_ref, ng_ref,
                ob_ref, convout_ref, sout_ref, xbuf, act, s_scr, *, c, cpt):
    t = pl.program_id(1)
    tt = c * cpt
    pad = 8

    @pl.when(t == 0)
    def _():
        xbuf[pad - (CONV_W - 1):pad, :] = conv0_ref[...]
        s_scr[...] = s0_ref[...]

    xbuf[pad:pad + tt, :] = xc_ref[...]
    yc = xbuf[pad - 3:pad - 3 + tt, :] * convw_ref[0:1, :]
    for j in range(1, CONV_W):
        yc = yc + xbuf[pad - 3 + j:pad - 3 + j + tt, :] * convw_ref[j:j + 1, :]
    act[...] = yc * _sigmoid(yc)
    convout_ref[...] = xbuf[pad + tt - (CONV_W - 1):pad + tt, :]
    xbuf[0:pad, :] = xbuf[tt:tt + pad, :]

    m = misc_ref[...]
    beta_all = _sigmoid(m)
    g_all = -jnp.exp(alog_ref[...]) * _softplus(m + dtb_ref[...])
    ri = lax.broadcasted_iota(I32, (tt, tt), 0)
    ci = lax.broadcasted_iota(I32, (tt, tt), 1)
    same_chunk_prefix = ((ri // c) == (ci // c)) & (ci <= ri)
    gc_all = _dot(same_chunk_prefix.astype(F32), g_all, HIGHEST)

    r = lax.broadcasted_iota(I32, (c, c), 0)
    q_ = lax.broadcasted_iota(I32, (c, c), 1)
    tri = q_ <= r
    strict = q_ < r
    l_incl = tri.astype(F32)
    u_strict = (r > q_).astype(F32)

    for ch in range(cpt):
        rs = slice(ch * c, (ch + 1) * c)
        for h in range(N_HEADS_B):
            q = act[rs, h * DK_B:(h + 1) * DK_B]
            k = act[rs, WIDTH_QK_B + h * DK_B:WIDTH_QK_B + (h + 1) * DK_B]
            v = act[rs, 2 * WIDTH_QK_B + h * DV_B:2 * WIDTH_QK_B + (h + 1) * DV_B]
            q = q * lax.rsqrt(jnp.sum(q * q, axis=-1, keepdims=True) + EPS) * (DK_B ** -0.5)
            k = k * lax.rsqrt(jnp.sum(k * k, axis=-1, keepdims=True) + EPS)
            b = beta_all[rs, MISC_BETA + h:MISC_BETA + h + 1]
            g = g_all[rs, MISC_A + h:MISC_A + h + 1]
            gc = gc_all[rs, MISC_A + h:MISC_A + h + 1]
            diff = _dot(l_incl, g * u_strict, HIGHEST)
            decay = jnp.exp(jnp.where(tri, diff, -jnp.inf))
            eg = jnp.exp(gc)
            kb = k * b
            a = jnp.where(strict, _dot_nt(kb, k, HIGHEST) * decay, 0.0)
            tinv = _unit_lower_inverse(a, c)
            u = _dot(tinv, v * b, HIGHEST)
            w = _dot(tinv, kb * eg, HIGHEST)
            attn = jnp.where(tri, _dot_nt(q, k, HIGHEST) * decay, 0.0)
            s = s_scr[h]
            v_new = u - _dot(w, s, HIGHEST)
            o = _dot(q * eg, s, HIGHEST) + _dot(attn, v_new, HIGHEST)
            g_last = gc[c - 1:c, :]
            s_scr[h] = s * jnp.exp(g_last) + _dot_tn(k * jnp.exp(g_last - gc), v_new, HIGHEST)
            on = o * lax.rsqrt(jnp.mean(o * o, axis=-1, keepdims=True) + EPS) * ng_ref[...]
            zz = z_ref[rs, h * DV_B:(h + 1) * DV_B]
            ob_ref[rs, h * DV_B:(h + 1) * DV_B] = (on * (zz * _sigmoid(zz))).astype(ob_ref.dtype)

    @pl.when(t == pl.num_programs(1) - 1)
    def _():
        sout_ref[...] = s_scr[...]


def _gdn_call(xc, z, misc, conv0, s0, convw, alog_pad, dtb_pad, ng, c, cpt):
    b, t, _ = xc.shape
    tt = c * cpt
    tok = lambda w: pl.BlockSpec((None, tt, w), lambda i, j: (i, j, 0))
    per_b = lambda shp: pl.BlockSpec((None,) + shp, lambda i, j: (i,) + (0,) * len(shp))
    return pl.pallas_call(
        functools.partial(_gdn_kernel, c=c, cpt=cpt),
        out_shape=[jax.ShapeDtypeStruct((b, t, WIDTH_V_B), BF16),
                   jax.ShapeDtypeStruct((b, CONV_W - 1, CONV_CH), F32),
                   jax.ShapeDtypeStruct((b, N_HEADS_B, DK_B, DV_B), F32)],
        grid=(b, t // tt),
        in_specs=[tok(CONV_CH), tok(WIDTH_V_B), tok(MISC_W), per_b((CONV_W - 1, CONV_CH)),
                  per_b((N_HEADS_B, DK_B, DV_B)),
                  pl.BlockSpec((CONV_W, CONV_CH), lambda i, j: (0, 0)),
                  pl.BlockSpec((1, MISC_W), lambda i, j: (0, 0)),
                  pl.BlockSpec((1, MISC_W), lambda i, j: (0, 0)),
                  pl.BlockSpec((1, DV_B), lambda i, j: (0, 0))],
        out_specs=[tok(WIDTH_V_B), per_b((CONV_W - 1, CONV_CH)), per_b((N_HEADS_B, DK_B, DV_B))],
        scratch_shapes=[pltpu.VMEM((tt + 8, CONV_CH), F32), pltpu.VMEM((tt, CONV_CH), F32),
                        pltpu.VMEM((N_HEADS_B, DK_B, DV_B), F32)],
        compiler_params=_params(2),
        name="gated_deltanet",
    )(xc, z, misc, conv0, s0, convw, alog_pad, dtb_pad, ng)


def _score_key(score):
    bits = pltpu.bitcast(score, I32)
    mag = bits & INT_MAX
    return jnp.where(bits < 0, -mag, mag)


def _dsa_kernel(qa_ref, qi_ref, misc_ref, k_ref, v_ref, ki_ref, o_ref, keys_scr,
                *, tq, kt, past, n_keys, n_sel):
    q0 = pl.program_id(1) * tq
    row = lax.broadcasted_iota(I32, (tq, 1), 0)
    pos = past + q0 + row
    limit = jnp.minimum((lax.shift_right_logical(pos, 6) + 1) * CHUNK, n_keys)
    max_limit = jnp.minimum((lax.shift_right_logical(past + q0 + tq - 1, 6) + 1) * CHUNK, n_keys)
    nkt = lax.div(max_limit + kt - 1, kt)
    kkf = jnp.minimum(n_sel, limit).astype(F32)

    misc = misc_ref[...]
    qi = (qi_ref[...] * (IDX_DIM ** -0.5)).astype(BF16)
    qi_hm = jnp.concatenate([qi[:, h * IDX_DIM:(h + 1) * IDX_DIM] for h in range(N_IDX_HEADS)], axis=0)
    wis = [misc[:, MISC_WI + h:MISC_WI + h + 1] * (N_IDX_HEADS ** -0.5) for h in range(N_IDX_HEADS)]
    lane = lax.broadcasted_iota(I32, (tq, kt), 1)

    def score_tile(j, carry):
        ki = ki_ref[pl.ds(pl.multiple_of(j * kt, kt), kt), :]
        logits = _dot_nt(qi_hm, ki)
        score = jnp.maximum(logits[0:tq], 0.0) * wis[0]
        for h in range(1, N_IDX_HEADS):
            score = score + jnp.maximum(logits[h * tq:(h + 1) * tq], 0.0) * wis[h]
        admissible = (lane + j * kt) < limit
        keys_scr[j] = jnp.where(admissible, _score_key(score), INT_MIN)
        return carry

    lax.fori_loop(0, nkt, score_tile, 0)

    def count_ge(thr):
        def body(j, acc):
            c = jnp.where(keys_scr[j] >= thr, 1.0, 0.0)
            part = c[:, 0:LANES]
            for s in range(1, kt // LANES):
                part = part + c[:, s * LANES:(s + 1) * LANES]
            return acc + part
        acc = lax.fori_loop(0, nkt, body, jnp.zeros((tq, LANES), F32))
        return jnp.sum(acc, axis=1, keepdims=True)

    def n_open(lo, hi):
        return jnp.max(jnp.where(hi != lo + 1, 1.0, 0.0))

    c1 = count_ge(jnp.full((tq, 1), 1, I32))
    c0 = count_ge(jnp.full((tq, 1), 0, I32))
    ge1 = c1 >= kkf
    ge0 = c0 >= kkf
    lo0 = jnp.where(ge1, 1, jnp.where(ge0, 0, INT_MIN))
    hi0 = jnp.where(ge1, INT_MAX, jnp.where(ge0, 1, 0))
    chi0 = jnp.where(ge1, 0.0, jnp.where(ge0, c1, c0))

    def bis_cond(st):
        return st[3] > 0.5

    def bis_body(st):
        lo, hi, chi, _ = st
        active = hi != lo + 1
        mid = (lo >> 1) + (hi >> 1) + (lo & hi & 1)
        c = count_ge(mid)
        ge = c >= kkf
        exact = active & (c == kkf)
        lo_n = jnp.where(active & ge, mid, lo)
        hi_n = jnp.where(exact, mid + 1, jnp.where(active & jnp.logical_not(ge), mid, hi))
        chi_n = jnp.where(exact, NEG_BIG, jnp.where(active & jnp.logical_not(ge), c, chi))
        return lo_n, hi_n, chi_n, n_open(lo_n, hi_n)

    tau, _, cnt_gt, _ = lax.while_loop(bis_cond, bis_body, (lo0, hi0, chi0, n_open(lo0, hi0)))
    need = kkf - cnt_gt

    qa = qa_ref[...]
    zeros_half = jnp.zeros((GROUP_A * tq, HEAD_DIM_A), BF16)
    q_kv = []
    for n in range(N_KV_A):
        qn = jnp.concatenate([qa[:, (n * GROUP_A + g) * HEAD_DIM_A:(n * GROUP_A + g + 1) * HEAD_DIM_A]
                              for g in range(GROUP_A)], axis=0)
        parts = [zeros_half] * N_KV_A
        parts[n] = qn
        q_kv.append(jnp.concatenate(parts, axis=1))
    ur = lax.broadcasted_iota(I32, (kt, kt), 0)
    uc = lax.broadcasted_iota(I32, (kt, kt), 1)
    before = (ur < uc).astype(BF16)
    mrows = GROUP_A * tq

    def attn_tile(j, carry):
        tie_seen, stats = carry
        key = keys_scr[j]
        eq = key == tau
        eqf = jnp.where(eq, 1.0, 0.0)
        rank = tie_seen + _dot(eqf.astype(BF16), before)
        sel = (key > tau) | (eq & (rank < need))
        sel4 = jnp.concatenate([sel] * GROUP_A, axis=0)
        start = pl.multiple_of(j * kt, kt)
        kk = k_ref[pl.ds(start, kt), :]
        vv = v_ref[pl.ds(start, kt), :]
        new_stats = []
        for n in range(N_KV_A):
            m, l, acc = stats[n]
            s = jnp.where(sel4, _dot_nt(q_kv[n], kk), NEG_BIG)
            m_new = jnp.maximum(m, jnp.max(s, axis=1, keepdims=True))
            p = jnp.exp(s - m_new)
            alpha = jnp.exp(m - m_new)
            l_new = alpha * l + jnp.sum(p, axis=1, keepdims=True)
            acc_new = alpha * acc + _dot(p.astype(BF16), vv)
            new_stats.append((m_new, l_new, acc_new))
        return tie_seen + jnp.sum(eqf, axis=1, keepdims=True), tuple(new_stats)

    init = (jnp.zeros((tq, 1), F32),
            tuple((jnp.full((mrows, 1), NEG_BIG, F32), jnp.zeros((mrows, 1), F32),
                   jnp.zeros((mrows, WIDTH_KV_A), F32)) for _ in range(N_KV_A)))
    _, stats = lax.fori_loop(0, nkt, attn_tile, init)

    pieces = []
    for n in range(N_KV_A):
        _, l, acc = stats[n]
        on = acc / l
        for g in range(GROUP_A):
            pieces.append(on[g * tq:(g + 1) * tq, n * HEAD_DIM_A:(n + 1) * HEAD_DIM_A])
    o_ref[...] = jnp.concatenate(pieces, axis=1).astype(o_ref.dtype)


def _dsa_call(qa, qi, misc, kb, vb, kib, tq, kt, past, n_keys):
    b, t, _ = qa.shape
    lp = kb.shape[1]
    n_sel = min(TOPK_MAX, n_keys // 4)
    qtok = lambda w: pl.BlockSpec((None, tq, w), lambda i, j: (i, j, 0))
    keys = lambda w: pl.BlockSpec((None, lp, w), lambda i, j: (i, 0, 0))
    return pl.pallas_call(
        functools.partial(_dsa_kernel, tq=tq, kt=kt, past=past, n_keys=n_keys, n_sel=n_sel),
        out_shape=jax.ShapeDtypeStruct((b, t, WIDTH_A), BF16),
        grid=(b, t // tq),
        in_specs=[qtok(WIDTH_A), qtok(WIDTH_IDX_Q), qtok(MISC_W), keys(WIDTH_KV_A), keys(WIDTH_KV_A), keys(IDX_DIM)],
        out_specs=qtok(WIDTH_A),
        scratch_shapes=[pltpu.VMEM((lp // kt, tq, kt), I32)],
        compiler_params=_params(2),
        name="dsa",
    )(qa, qi, misc, kb, vb, kib)


def _pick_tile(n, pref):
    t = min(n, pref)
    while n % t:
        t //= 2
    return t


def _pad_keys(a, lp):
    return jnp.pad(a, ((0, 0), (0, lp - a.shape[1]), (0, 0)))


def _layer(x, w, past, final_norm, final):
    b, t, d = x.shape
    n = b * t
    tm = _pick_tile(n, 256)
    x_flat = x.reshape(n, d)
    x1 = _ffn_call(x_flat, w["ffn1_norm"], w["ffn1_w_gate"], w["ffn1_w_up"], w["ffn1_w_down"], tm, 2)
    (qa, k, v, kb, vb, qi, ki, kib, misc, xc, z, ga, gb) = _proj_call(x1, w["mix_norm"], w["w_in_groups"], tm)
    per_batch = lambda a: a.reshape(b, t, a.shape[-1])

    if past is None:
        conv0 = jnp.zeros((b, CONV_W - 1, CONV_CH), F32)
        s0 = jnp.zeros((b, N_HEADS_B, DK_B, DV_B), F32)
        chunk, past_len = CHUNK, 0
        kb_all, vb_all, kib_all = per_batch(kb), per_batch(vb), per_batch(kib)
    else:
        ck, cv, cki, conv0, s0 = past
        chunk, past_len = t, ck.shape[1]
        kb_all = jnp.concatenate([ck.reshape(b, past_len, WIDTH_KV_A).astype(BF16), per_batch(kb)], axis=1)
        vb_all = jnp.concatenate([cv.reshape(b, past_len, WIDTH_KV_A).astype(BF16), per_batch(vb)], axis=1)
        kib_all = jnp.concatenate([cki.astype(BF16), per_batch(kib)], axis=1)
    n_keys = past_len + t
    kt = 256
    lp = -(-n_keys // kt) * kt
    if lp != n_keys:
        kb_all, vb_all, kib_all = _pad_keys(kb_all, lp), _pad_keys(vb_all, lp), _pad_keys(kib_all, lp)

    cpt = max(1, _pick_tile(t, 128) // chunk)
    ob, conv_new, s_new = _gdn_call(per_batch(xc), per_batch(z), per_batch(misc), conv0, s0, w["conv_w"],
                                    w["alog_pad"], w["dtb_pad"], w["delta_norm"], chunk, cpt)
    tq = _pick_tile(t, 128)
    oa = _dsa_call(per_batch(qa), per_batch(qi), per_batch(misc), kb_all, vb_all, kib_all, tq, kt, past_len, n_keys)

    y = _out_call(x1, oa.reshape(n, WIDTH_A), ob.reshape(n, WIDTH_V_B), ga, gb,
                  w["w_branch_a"], w["w_branch_b"], w["w_out"],
                  w["ffn2_norm"], w["ffn2_w_gate"], w["ffn2_w_up"], w["ffn2_w_down"], final_norm, final, tm, 2)
    return (y.reshape(b, t, d),
            (k.reshape(b, t, N_KV_A, HEAD_DIM_A), v.reshape(b, t, N_KV_A, HEAD_DIM_A), per_batch(ki),
             conv_new, s_new))


def _split_w_in(w_in):
    d = w_in.shape[0]
    sizes = (WIDTH_A, WIDTH_KV_A, WIDTH_KV_A, WIDTH_IDX_Q, IDX_DIM, N_IDX_HEADS,
             WIDTH_QK_B, WIDTH_QK_B, WIDTH_V_B, WIDTH_V_B, N_HEADS_B, N_HEADS_B, d, d)
    offs = [0]
    for s in sizes:
        offs.append(offs[-1] + s)
    col = lambda i: w_in[:, offs[i]:offs[i + 1]]
    qa, ka, va, qi, ki, wi, qb, kb_, vb, zb, bb, ab, gate_a, gate_b = [col(i) for i in range(len(sizes))]
    misc = jnp.zeros((d, MISC_W), w_in.dtype)
    misc = misc.at[:, 0:IDX_DIM].set(ki)
    misc = misc.at[:, MISC_WI:MISC_WI + N_IDX_HEADS].set(wi)
    misc = misc.at[:, MISC_BETA:MISC_BETA + N_HEADS_B].set(bb)
    misc = misc.at[:, MISC_A:MISC_A + N_HEADS_B].set(ab)
    xc = jnp.concatenate([qb, kb_, vb], axis=1)
    return [m.astype(BF16) for m in (qa, ka, va, qi, misc, xc, zb, gate_a, gate_b)]


def _lane_pad(vec, start):
    return jnp.zeros((1, MISC_W), F32).at[0, start:start + vec.shape[0]].set(vec)


def kernel(x_prompt, x_sample, cache_attn_k, cache_attn_v, cache_idx_k, state_conv, state_delta,
           ffn1_norm, ffn1_w_gate, ffn1_w_up, ffn1_w_down, mix_norm, w_in, conv_w, a_log, dt_bias,
           delta_norm, w_branch_a, w_branch_b, w_out, ffn2_norm, ffn2_w_gate, ffn2_w_up, ffn2_w_down,
           final_norm):
    depth = ffn1_norm.shape[0]
    xp, xs = x_prompt, x_sample
    st_p, st_s = [], []
    row = lambda a: a.reshape(1, -1)
    for l in range(depth):
        w = {
            "ffn1_norm": row(ffn1_norm[l]), "ffn1_w_gate": ffn1_w_gate[l].astype(BF16),
            "ffn1_w_up": ffn1_w_up[l].astype(BF16), "ffn1_w_down": ffn1_w_down[l].astype(BF16),
            "mix_norm": row(mix_norm[l]), "w_in_groups": _split_w_in(w_in[l]), "conv_w": conv_w[l],
            "alog_pad": _lane_pad(a_log[l], MISC_A), "dtb_pad": _lane_pad(dt_bias[l], MISC_A),
            "delta_norm": row(delta_norm[l]),
            "w_branch_a": w_branch_a[l].astype(BF16), "w_branch_b": w_branch_b[l].astype(BF16),
            "w_out": w_out[l].astype(BF16),
            "ffn2_norm": row(ffn2_norm[l]), "ffn2_w_gate": ffn2_w_gate[l].astype(BF16),
            "ffn2_w_up": ffn2_w_up[l].astype(BF16), "ffn2_w_down": ffn2_w_down[l].astype(BF16),
        }
        last = l == depth - 1
        fin = row(final_norm)
        xp, sp = _layer(xp, w, None, fin, last)
        xs, ss = _layer(xs, w, (cache_attn_k[l], cache_attn_v[l], cache_idx_k[l], state_conv[l], state_delta[l]),
                        fin, last)
        st_p.append(sp)
        st_s.append(ss)

    def stacked(states, i):
        return jnp.stack([s[i] for s in states], axis=0)

    return (xp, xs,
            stacked(st_p, 0), stacked(st_p, 1), stacked(st_p, 2), stacked(st_p, 3), stacked(st_p, 4),
            stacked(st_s, 0), stacked(st_s, 1), stacked(st_s, 2), stacked(st_s, 3), stacked(st_s, 4))
```

```python
import functools
import math

import jax
import jax.numpy as jnp
from jax import lax
from jax.experimental import pallas as pl
from jax.experimental.pallas import tpu as pltpu

F32 = jnp.float32
BF16 = jnp.bfloat16
I32 = jnp.int32

CHUNK = 64
N_HEADS_A = 8
N_KV_A = 2
GROUP_A = N_HEADS_A // N_KV_A
HEAD_DIM_A = 64
N_IDX_HEADS = 4
IDX_DIM = 64
TOPK_MAX = 256
N_HEADS_B = 4
DK_B = 128
DV_B = 128
CONV_W = 4
EPS = 1e-6

WIDTH_A = N_HEADS_A * HEAD_DIM_A
WIDTH_KV_A = N_KV_A * HEAD_DIM_A
WIDTH_IDX_Q = N_IDX_HEADS * IDX_DIM
WIDTH_QK_B = N_HEADS_B * DK_B
WIDTH_V_B = N_HEADS_B * DV_B
CONV_CH = 2 * WIDTH_QK_B + WIDTH_V_B

MISC_W = 128
MISC_WI = IDX_DIM
MISC_BETA = MISC_WI + N_IDX_HEADS
MISC_A = 72

V7X_VMEM_LIMIT = 56 * 1024 * 1024
LANES = 128
WIDTH_A_PAD = N_HEADS_A * LANES
INT_MIN = -2 ** 31
INT_MAX = 2 ** 31 - 1
NEG_BIG = -1e30
LOG2E = math.log2(math.e)

HIGHEST = lax.Precision.HIGHEST


def _sigmoid(x):
    return 1.0 / (1.0 + jnp.exp(-x))


def _softplus(x):
    return jnp.maximum(x, 0.0) + jnp.log(1.0 + jnp.exp(-jnp.abs(x)))


def _rms(x, g):
    return x * lax.rsqrt(jnp.mean(x * x, axis=-1, keepdims=True) + EPS) * g


def _dot(a, b, precision=None):
    return jnp.dot(a, b, preferred_element_type=F32, precision=precision)


def _dot_nt(a, b, precision=None):
    return lax.dot_general(a, b, (((1,), (1,)), ((), ())), preferred_element_type=F32, precision=precision)


def _dot_tn(a, b, precision=None):
    return lax.dot_general(a, b, (((0,), (0,)), ((), ())), preferred_element_type=F32, precision=precision)


def _mm(a, b):
    return _dot(a.astype(BF16), b.astype(BF16))


def _mm_nt(a, b):
    return _dot_nt(a.astype(BF16), b.astype(BF16))


def _mm_tn(a, b):
    return _dot_tn(a.astype(BF16), b.astype(BF16))


def _resident(shape):
    nd = len(shape)
    return pl.BlockSpec(shape, lambda *_: (0,) * nd, pipeline_mode=pl.Buffered(1))


def _params(n_axes):
    return pltpu.CompilerParams(dimension_semantics=("arbitrary",) * n_axes, vmem_limit_bytes=V7X_VMEM_LIMIT)


def _swiglu_residual(x, g_ref, wg_ref, wu_ref, wd_ref, n_fc):
    hb = _rms(x, g_ref[...]).astype(BF16)
    fc = wg_ref.shape[1] // n_fc
    acc = None
    for c in range(n_fc):
        sl = slice(c * fc, (c + 1) * fc)
        gate = _dot(hb, wg_ref[:, sl])
        up = _dot(hb, wu_ref[:, sl])
        a = (gate * _sigmoid(gate) * up).astype(BF16)
        part = _dot(a, wd_ref[sl, :])
        acc = part if acc is None else acc + part
    return x + 0.5 * acc


def _ffn_kernel(x_ref, g_ref, wg_ref, wu_ref, wd_ref, o_ref, *, n_fc):
    o_ref[...] = _swiglu_residual(x_ref[...], g_ref, wg_ref, wu_ref, wd_ref, n_fc)


def _ffn_call(x, g, wg, wu, wd, tm, n_fc):
    n, d = x.shape
    f = wg.shape[1]
    return pl.pallas_call(
        functools.partial(_ffn_kernel, n_fc=n_fc),
        out_shape=jax.ShapeDtypeStruct((n, d), F32),
        grid=(n // tm,),
        in_specs=[pl.BlockSpec((tm, d), lambda i: (i, 0)), _resident((1, d)),
                  _resident((d, f)), _resident((d, f)), _resident((f, d))],
        out_specs=pl.BlockSpec((tm, d), lambda i: (i, 0)),
        compiler_params=_params(1),
        name="ffn1",
    )(x, g, wg, wu, wd)


def _out_kernel(x_ref, oa_ref, ob_ref, ga_ref, gb_ref, wa_ref, wb_ref, wo_ref,
                g_ref, wg_ref, wu_ref, wd_ref, gf_ref, o_ref, *, n_fc, final):
    ya = _dot(oa_ref[...], wa_ref[...])
    yb = _dot(ob_ref[...], wb_ref[...])
    merged = _sigmoid(ga_ref[...]) * ya + _sigmoid(gb_ref[...]) * yb
    x2 = x_ref[...] + _dot(merged.astype(BF16), wo_ref[...])
    x3 = _swiglu_residual(x2, g_ref, wg_ref, wu_ref, wd_ref, n_fc)
    o_ref[...] = _rms(x3, gf_ref[...]) if final else x3


def _out_call(x1, oa, ob, ga, gb, wa, wb, wo, g, wg, wu, wd, gf, final, tm, n_fc):
    n, d = x1.shape
    f = wg.shape[1]
    tok = lambda w: pl.BlockSpec((tm, w), lambda i: (i, 0))
    return pl.pallas_call(
        functools.partial(_out_kernel, n_fc=n_fc, final=final),
        out_shape=jax.ShapeDtypeStruct((n, d), F32),
        grid=(n // tm,),
        in_specs=[tok(d), tok(WIDTH_A_PAD), tok(WIDTH_V_B), tok(d), tok(d),
                  _resident((WIDTH_A_PAD, d)), _resident((WIDTH_V_B, d)), _resident((d, d)),
                  _resident((1, d)), _resident((d, f)), _resident((d, f)), _resident((f, d)),
                  _resident((1, d))],
        out_specs=tok(d),
        compiler_params=_params(1),
        name="merge_ffn2",
    )(x1, oa, ob, ga, gb, wa, wb, wo, g, wg, wu, wd, gf)


def _proj_kernel(x_ref, g_ref, wqa_ref, wk_ref, wv_ref, wv0_ref, wv1_ref, wqi_ref, wmisc_ref, wxc_ref, wz_ref,
                 wga_ref, wgb_ref,
                 qa_ref, k_ref, v_ref, kb_ref, v0_ref, v1_ref, qi_ref, ki_ref, kib_ref, misc_ref, xc_ref, z_ref,
                 ga_ref, gb_ref):
    hb = _rms(x_ref[...], g_ref[...]).astype(BF16)
    qa_ref[...] = (_dot(hb, wqa_ref[...]) * (HEAD_DIM_A ** -0.5 * LOG2E)).astype(BF16)
    k = _dot(hb, wk_ref[...])
    k_ref[...] = k
    kb_ref[...] = k.astype(BF16)
    v_ref[...] = _dot(hb, wv_ref[...])
    ones_lane = (lax.broadcasted_iota(I32, (1, LANES), 1) == HEAD_DIM_A).astype(F32)
    v0_ref[...] = (_dot(hb, wv0_ref[...]) + ones_lane).astype(BF16)
    v1_ref[...] = (_dot(hb, wv1_ref[...]) + ones_lane).astype(BF16)
    qi_ref[...] = _dot(hb, wqi_ref[...])
    misc = _dot(hb, wmisc_ref[...])
    misc_ref[...] = misc
    ki = misc[:, :IDX_DIM]
    ki_ref[...] = ki
    kib_ref[...] = ki.astype(BF16)
    xc_ref[...] = _dot(hb, wxc_ref[...])
    z_ref[...] = _dot(hb, wz_ref[...])
    ga_ref[...] = _dot(hb, wga_ref[...])
    gb_ref[...] = _dot(hb, wgb_ref[...])


def _proj_call(x1, g, ws, tm):
    n, d = x1.shape
    widths = [(WIDTH_A_PAD, BF16), (WIDTH_KV_A, F32), (WIDTH_KV_A, F32), (WIDTH_KV_A, BF16),
              (LANES, BF16), (LANES, BF16),
              (WIDTH_IDX_Q, F32), (IDX_DIM, F32), (IDX_DIM, BF16), (MISC_W, F32), (CONV_CH, F32),
              (WIDTH_V_B, F32), (d, F32), (d, F32)]
    tok = lambda w: pl.BlockSpec((tm, w), lambda i: (i, 0))
    return pl.pallas_call(
        _proj_kernel,
        out_shape=[jax.ShapeDtypeStruct((n, w), dt) for w, dt in widths],
        grid=(n // tm,),
        in_specs=[tok(d), _resident((1, d))] + [_resident(w.shape) for w in ws],
        out_specs=[tok(w) for w, _ in widths],
        compiler_params=_params(1),
        name="in_proj",
    )(x1, g, *ws)


def _unit_lower_inverse(a, nilpotent):
    size = a.shape[0]
    n = -a
    eye = (lax.broadcasted_iota(I32, (size, size), 0) == lax.broadcasted_iota(I32, (size, size), 1)).astype(F32)
    t = eye + n
    p = n
    span = 2
    while span < nilpotent:
        p = _mm(p, p)
        t = t + _mm(t, p)
        span *= 2
    return t


def _gdn_kernel(xc_ref, z_ref, misc_ref, conv0_ref, s0_ref, convw_ref, alog_ref, dtb_ref, ng_ref,
                ob_ref, convout_ref, sout_ref, xbuf, act, s_scr, *, c, cpt):
    t = pl.program_id(1)
    tt = c * cpt
    pad = 8

    @pl.when(t == 0)
    def _():
        xbuf[pad - (CONV_W - 1):pad, :] = conv0_ref[...]
        s_scr[...] = s0_ref[...]

    xbuf[pad:pad + tt, :] = xc_ref[...]
    yc = xbuf[pad - 3:pad - 3 + tt, :] * convw_ref[0:1, :]
    for j in range(1, CONV_W):
        yc = yc + xbuf[pad - 3 + j:pad - 3 + j + tt, :] * convw_ref[j:j + 1, :]
    act[...] = yc * _sigmoid(yc)
    convout_ref[...] = xbuf[pad + tt - (CONV_W - 1):pad + tt, :]
    xbuf[0:pad, :] = xbuf[tt:tt + pad, :]

    m = misc_ref[...]
    beta_all = _sigmoid(m)
    g_all = -jnp.exp(alog_ref[...]) * _softplus(m + dtb_ref[...])
    ri = lax.broadcasted_iota(I32, (tt, tt), 0)
    ci = lax.broadcasted_iota(I32, (tt, tt), 1)
    same_chunk_prefix = ((ri // c) == (ci // c)) & (ci <= ri)
    gc_all = _dot(same_chunk_prefix.astype(F32), g_all, HIGHEST)
    pick = (lax.broadcasted_iota(I32, (8, MISC_W), 1) == lax.broadcasted_iota(I32, (8, MISC_W), 0) + MISC_A)
    gc_rows = _dot_nt(pick.astype(F32), gc_all, HIGHEST)

    nh = N_HEADS_B
    n = nh * c
    ri = lax.broadcasted_iota(I32, (n, n), 0)
    ci = lax.broadcasted_iota(I32, (n, n), 1)
    same_head = (ri // c) == (ci // c)
    tri = same_head & (ci <= ri)
    strict = same_head & (ci < ri)

    for ch in range(cpt):
        rs = slice(ch * c, (ch + 1) * c)
        stack = lambda ref, base, w: jnp.concatenate([ref[rs, base + h * w:base + (h + 1) * w] for h in range(nh)], axis=0)
        col = lambda arr, lane0: jnp.concatenate([arr[rs, lane0 + h:lane0 + h + 1] for h in range(nh)], axis=0)
        q = stack(act, 0, DK_B)
        k = stack(act, WIDTH_QK_B, DK_B)
        v = stack(act, 2 * WIDTH_QK_B, DV_B)
        q = q * lax.rsqrt(jnp.sum(q * q, axis=-1, keepdims=True) + EPS) * (DK_B ** -0.5)
        k = k * lax.rsqrt(jnp.sum(k * k, axis=-1, keepdims=True) + EPS)
        b = col(beta_all, MISC_BETA)
        gc = col(gc_all, MISC_A)
        gc_row = jnp.concatenate([gc_rows[h:h + 1, rs] for h in range(nh)], axis=1)
        g_last = jnp.concatenate([jnp.broadcast_to(gc[(h + 1) * c - 1:(h + 1) * c, :], (c, 1))
                                  for h in range(nh)], axis=0)
        decay = jnp.exp(jnp.where(tri, gc - gc_row, -jnp.inf))
        eg = jnp.exp(gc)
        kb = k * b
        a = jnp.where(strict, _mm_nt(kb, k) * decay, 0.0)
        tinv = _unit_lower_inverse(a, c)
        sol = _mm(tinv, jnp.concatenate([v * b, kb * eg], axis=1))
        u = sol[:, :DV_B]
        w = sol[:, DV_B:]
        attn = jnp.where(tri, _mm_nt(q, k) * decay, 0.0)
        qe = q * eg
        kd = k * jnp.exp(g_last - gc)
        v_new, o_state = [], []
        for h in range(nh):
            hs = slice(h * c, (h + 1) * c)
            s = s_scr[h]
            ws = _mm(jnp.concatenate([w[hs], qe[hs]], axis=0), s)
            vn = u[hs] - ws[:c]
            v_new.append(vn)
            o_state.append(ws[c:])
            s_scr[h] = s * jnp.exp(g_last[hs][0:1, :]) + _mm_tn(kd[hs], vn)
        o = jnp.concatenate(o_state, axis=0) + _mm(attn, jnp.concatenate(v_new, axis=0))
        on = o * lax.rsqrt(jnp.mean(o * o, axis=-1, keepdims=True) + EPS) * ng_ref[...]
        zz = stack(z_ref, 0, DV_B)
        gated = (on * (zz * _sigmoid(zz))).astype(ob_ref.dtype)
        for h in range(nh):
            ob_ref[rs, h * DV_B:(h + 1) * DV_B] = gated[h * c:(h + 1) * c]

    @pl.when(t == pl.num_programs(1) - 1)
    def _():
        sout_ref[...] = s_scr[...]


def _gdn_call(xc, z, misc, conv0, s0, convw, alog_pad, dtb_pad, ng, c, cpt):
    b, t, _ = xc.shape
    tt = c * cpt
    tok = lambda w: pl.BlockSpec((None, tt, w), lambda i, j: (i, j, 0))
    per_b = lambda shp: pl.BlockSpec((None,) + shp, lambda i, j: (i,) + (0,) * len(shp))
    return pl.pallas_call(
        functools.partial(_gdn_kernel, c=c, cpt=cpt),
        out_shape=[jax.ShapeDtypeStruct((b, t, WIDTH_V_B), BF16),
                   jax.ShapeDtypeStruct((b, CONV_W - 1, CONV_CH), F32),
                   jax.ShapeDtypeStruct((b, N_HEADS_B, DK_B, DV_B), F32)],
        grid=(b, t // tt),
        in_specs=[tok(CONV_CH), tok(WIDTH_V_B), tok(MISC_W), per_b((CONV_W - 1, CONV_CH)),
                  per_b((N_HEADS_B, DK_B, DV_B)),
                  pl.BlockSpec((CONV_W, CONV_CH), lambda i, j: (0, 0)),
                  pl.BlockSpec((1, MISC_W), lambda i, j: (0, 0)),
                  pl.BlockSpec((1, MISC_W), lambda i, j: (0, 0)),
                  pl.BlockSpec((1, DV_B), lambda i, j: (0, 0))],
        out_specs=[tok(WIDTH_V_B), per_b((CONV_W - 1, CONV_CH)), per_b((N_HEADS_B, DK_B, DV_B))],
        scratch_shapes=[pltpu.VMEM((tt + 8, CONV_CH), F32), pltpu.VMEM((tt, CONV_CH), F32),
                        pltpu.VMEM((N_HEADS_B, DK_B, DV_B), F32)],
        compiler_params=_params(2),
        name="gated_deltanet",
    )(xc, z, misc, conv0, s0, convw, alog_pad, dtb_pad, ng)


def _score_key(score):
    bits = pltpu.bitcast(score, I32)
    mag = bits & INT_MAX
    return jnp.where(bits < 0, -mag, mag)


def _key_score(key):
    bits = jnp.where(key < 0, (-key) | INT_MIN, key)
    return pltpu.bitcast(bits, F32)


def _dsa_kernel(qa_ref, qi_ref, misc_ref, k_ref, v0_ref, v1_ref, ki_ref, o_ref, sc_scr, m_scr, acc_scr,
                *, tq, kt, past, n_keys, n_sel):
    q0 = pl.program_id(1) * tq
    row = lax.broadcasted_iota(I32, (tq, 1), 0)
    pos = past + q0 + row

    def key_limit(p):
        return jnp.minimum((lax.shift_right_logical(p, 6) + 1) * CHUNK, n_keys)

    limit = key_limit(pos)
    nkt = lax.div(key_limit(past + q0 + tq - 1) + kt - 1, kt)
    n_full = lax.div(key_limit(past + q0), kt)
    kkf = jnp.minimum(n_sel, limit).astype(F32)

    misc = misc_ref[...]
    qi = (qi_ref[...] * (IDX_DIM ** -0.5)).astype(BF16)
    qi_h = [qi[:, h * IDX_DIM:(h + 1) * IDX_DIM] for h in range(N_IDX_HEADS)]
    wis = [misc[:, MISC_WI + h:MISC_WI + h + 1] * (N_IDX_HEADS ** -0.5) for h in range(N_IDX_HEADS)]

    def score_tile(j, masked):
        ki = ki_ref[pl.ds(pl.multiple_of(j * kt, kt), kt), :]
        score = None
        for h in range(N_IDX_HEADS):
            term = jnp.maximum(_dot_nt(qi_h[h], ki), 0.0) * wis[h]
            score = term if score is None else score + term
        if masked:
            lane = lax.broadcasted_iota(I32, (tq, kt), 1)
            score = jnp.where(lane + j * kt < limit, score, -jnp.inf)
        sc_scr[j] = score

    def full_tiles(j, carry):
        score_tile(j, False)
        return carry

    def edge_tiles(j, carry):
        score_tile(j, True)
        return carry

    lax.fori_loop(0, n_full, full_tiles, 0)
    lax.fori_loop(n_full, nkt, edge_tiles, 0)

    def top2(j, carry):
        m1, m2 = carry
        x = sc_scr[j]
        for s in range(kt // LANES):
            xs = x[:, s * LANES:(s + 1) * LANES]
            m2 = jnp.maximum(m2, jnp.minimum(m1, xs))
            m1 = jnp.maximum(m1, xs)
        return m1, m2

    neg_inf = jnp.full((tq, LANES), -jnp.inf, F32)
    m1, m2 = lax.fori_loop(0, nkt, top2, (neg_inf, neg_inf))
    row_max = jnp.max(m1, axis=1, keepdims=True)
    floor = jnp.min(m2, axis=1, keepdims=True) if n_sel <= 2 * LANES else jnp.full((tq, 1), -jnp.inf, F32)
    lo0 = _score_key(floor)
    hi0 = _score_key(row_max) + 1

    def count_ge(thr):
        def body(j, acc):
            c = jnp.where(sc_scr[j] >= thr, 1.0, 0.0)
            part = c[:, 0:LANES]
            for s in range(1, kt // LANES):
                part = part + c[:, s * LANES:(s + 1) * LANES]
            return acc + part
        acc = lax.fori_loop(0, nkt, body, jnp.zeros((tq, LANES), F32))
        return jnp.sum(acc, axis=1, keepdims=True)

    def n_open(lo, hi):
        return jnp.max(jnp.where(hi != lo + 1, 1.0, 0.0))

    def bis_cond(st):
        return st[3] > 0.5

    def bis_body(st):
        lo, hi, chi, _ = st
        active = hi != lo + 1
        mid = (lo >> 1) + (hi >> 1) + (lo & hi & 1)
        mid = jnp.where((lo < 0) & (hi > 0), 0, jnp.where((lo == 0) & (hi > 1), 1, mid))
        c = count_ge(_key_score(mid))
        ge = c >= kkf
        lower = active & jnp.logical_not(ge)
        exact = active & (c == kkf)
        lo_n = jnp.where(active & ge, mid, lo)
        hi_n = jnp.where(exact, mid + 1, jnp.where(lower, mid, hi))
        chi_n = jnp.where(exact, NEG_BIG, jnp.where(lower, c, chi))
        return lo_n, hi_n, chi_n, n_open(lo_n, hi_n)

    tau_key, _, cnt_gt, _ = lax.while_loop(bis_cond, bis_body,
                                           (lo0, hi0, jnp.zeros((tq, 1), F32), n_open(lo0, hi0)))
    tau = _key_score(tau_key)
    need = kkf - cnt_gt

    ur = lax.broadcasted_iota(I32, (kt, kt), 0)
    uc = lax.broadcasted_iota(I32, (kt, kt), 1)
    before = (ur < uc).astype(BF16)
    m_scr[...] = jnp.full(m_scr.shape, NEG_BIG, F32)
    acc_scr[...] = jnp.zeros(acc_scr.shape, F32)
    v_refs = (v0_ref, v1_ref)

    def attn_tile(j, tie_seen):
        x = sc_scr[j]
        eq = x == tau
        eqf = jnp.where(eq, 1.0, 0.0)
        rank = tie_seen + _dot(eqf.astype(BF16), before)
        sel = (x > tau) | (eq & (rank < need))
        bias = jnp.where(sel, 0.0, NEG_BIG)
        start = pl.multiple_of(j * kt, kt)
        kk = k_ref[pl.ds(start, kt), :]
        for hq in range(N_HEADS_A):
            vv = v_refs[hq // GROUP_A][pl.ds(start, kt), :]
            s = _dot_nt(qa_ref[:, hq * LANES:(hq + 1) * LANES], kk) + bias
            m_old = m_scr[hq]
            m_new = jnp.maximum(m_old, jnp.max(s, axis=1, keepdims=True))
            p = jnp.exp2(s - jnp.concatenate([m_new] * (kt // LANES), axis=1))
            acc_scr[hq] = jnp.exp2(m_old - m_new) * acc_scr[hq] + _dot(p.astype(BF16), vv)
            m_scr[hq] = m_new
        return tie_seen + jnp.sum(eqf, axis=1, keepdims=True)

    lax.fori_loop(0, nkt, attn_tile, jnp.zeros((tq, 1), F32))

    for hq in range(N_HEADS_A):
        acc = acc_scr[hq]
        o_ref[:, hq * LANES:(hq + 1) * LANES] = (acc / acc[:, HEAD_DIM_A:HEAD_DIM_A + 1]).astype(o_ref.dtype)


def _dsa_call(qa, qi, misc, kb, v0, v1, kib, tq, kt, past, n_keys):
    b, t, _ = qa.shape
    lp = kb.shape[1]
    n_sel = min(TOPK_MAX, n_keys // 4)
    qtok = lambda w: pl.BlockSpec((None, tq, w), lambda i, j: (i, j, 0))
    keys = lambda w: pl.BlockSpec((None, lp, w), lambda i, j: (i, 0, 0))
    return pl.pallas_call(
        functools.partial(_dsa_kernel, tq=tq, kt=kt, past=past, n_keys=n_keys, n_sel=n_sel),
        out_shape=jax.ShapeDtypeStruct((b, t, WIDTH_A_PAD), BF16),
        grid=(b, t // tq),
        in_specs=[qtok(WIDTH_A_PAD), qtok(WIDTH_IDX_Q), qtok(MISC_W),
                  keys(WIDTH_KV_A), keys(LANES), keys(LANES), keys(IDX_DIM)],
        out_specs=qtok(WIDTH_A_PAD),
        scratch_shapes=[pltpu.VMEM((lp // kt, tq, kt), F32),
                        pltpu.VMEM((N_HEADS_A, tq, LANES), F32),
                        pltpu.VMEM((N_HEADS_A, tq, LANES), F32)],
        compiler_params=_params(2),
        name="dsa",
    )(qa, qi, misc, kb, v0, v1, kib)


def _pick_tile(n, pref):
    t = min(n, pref)
    while n % t:
        t //= 2
    return t


def _pad_keys(a, lp):
    return jnp.pad(a, ((0, 0), (0, lp - a.shape[1]), (0, 0)))


def _value_with_ones(v_head):
    b, l, _ = v_head.shape
    tail = jnp.zeros((b, l, LANES - HEAD_DIM_A), v_head.dtype).at[:, :, 0].set(1.0)
    return jnp.concatenate([v_head, tail], axis=-1).astype(BF16)


def _layer(x, w, past, final_norm, final):
    b, t, d = x.shape
    n = b * t
    tm = _pick_tile(n, 256)
    x_flat = x.reshape(n, d)
    x1 = _ffn_call(x_flat, w["ffn1_norm"], w["ffn1_w_gate"], w["ffn1_w_up"], w["ffn1_w_down"], tm, 2)
    (qa, k, v, kb, v0, v1, qi, ki, kib, misc, xc, z, ga, gb) = _proj_call(x1, w["mix_norm"], w["w_in_groups"], tm)
    per_batch = lambda a: a.reshape(b, t, a.shape[-1])

    if past is None:
        conv0 = jnp.zeros((b, CONV_W - 1, CONV_CH), F32)
        s0 = jnp.zeros((b, N_HEADS_B, DK_B, DV_B), F32)
        chunk, past_len = CHUNK, 0
        kb_all, v0_all, v1_all, kib_all = per_batch(kb), per_batch(v0), per_batch(v1), per_batch(kib)
    else:
        ck, cv, cki, conv0, s0 = past
        chunk, past_len = t, ck.shape[1]
        kb_all = jnp.concatenate([ck.reshape(b, past_len, WIDTH_KV_A).astype(BF16), per_batch(kb)], axis=1)
        v0_all = jnp.concatenate([_value_with_ones(cv[:, :, 0, :]), per_batch(v0)], axis=1)
        v1_all = jnp.concatenate([_value_with_ones(cv[:, :, 1, :]), per_batch(v1)], axis=1)
        kib_all = jnp.concatenate([cki.astype(BF16), per_batch(kib)], axis=1)
    n_keys = past_len + t
    kt = 256
    lp = -(-n_keys // kt) * kt
    if lp != n_keys:
        kb_all, v0_all, v1_all, kib_all = [_pad_keys(a, lp) for a in (kb_all, v0_all, v1_all, kib_all)]

    cpt = max(1, _pick_tile(t, 128) // chunk)
    ob, conv_new, s_new = _gdn_call(per_batch(xc), per_batch(z), per_batch(misc), conv0, s0, w["conv_w"],
                                    w["alog_pad"], w["dtb_pad"], w["delta_norm"], chunk, cpt)
    tq = _pick_tile(t, 128)
    oa = _dsa_call(per_batch(qa), per_batch(qi), per_batch(misc), kb_all, v0_all, v1_all, kib_all,
                   tq, kt, past_len, n_keys)

    y = _out_call(x1, oa.reshape(n, WIDTH_A_PAD), ob.reshape(n, WIDTH_V_B), ga, gb,
                  w["w_branch_a"], w["w_branch_b"], w["w_out"],
                  w["ffn2_norm"], w["ffn2_w_gate"], w["ffn2_w_up"], w["ffn2_w_down"], final_norm, final, tm, 2)
    return (y.reshape(b, t, d),
            (k.reshape(b, t, N_KV_A, HEAD_DIM_A), v.reshape(b, t, N_KV_A, HEAD_DIM_A), per_batch(ki),
             conv_new, s_new))


def _split_w_in(w_in):
    d = w_in.shape[0]
    sizes = (WIDTH_A, WIDTH_KV_A, WIDTH_KV_A, WIDTH_IDX_Q, IDX_DIM, N_IDX_HEADS,
             WIDTH_QK_B, WIDTH_QK_B, WIDTH_V_B, WIDTH_V_B, N_HEADS_B, N_HEADS_B, d, d)
    offs = [0]
    for s in sizes:
        offs.append(offs[-1] + s)
    col = lambda i: w_in[:, offs[i]:offs[i + 1]]
    qa, ka, va, qi, ki, wi, qb, kb_, vb, zb, bb, ab, gate_a, gate_b = [col(i) for i in range(len(sizes))]
    qa_pad = jnp.zeros((d, WIDTH_A_PAD), w_in.dtype)
    for hq in range(N_HEADS_A):
        dst = hq * LANES + (hq // GROUP_A) * HEAD_DIM_A
        qa_pad = qa_pad.at[:, dst:dst + HEAD_DIM_A].set(qa[:, hq * HEAD_DIM_A:(hq + 1) * HEAD_DIM_A])
    v_pads = [jnp.zeros((d, LANES), w_in.dtype).at[:, :HEAD_DIM_A].set(va[:, n * HEAD_DIM_A:(n + 1) * HEAD_DIM_A])
              for n in range(N_KV_A)]
    misc = jnp.zeros((d, MISC_W), w_in.dtype)
    misc = misc.at[:, 0:IDX_DIM].set(ki)
    misc = misc.at[:, MISC_WI:MISC_WI + N_IDX_HEADS].set(wi)
    misc = misc.at[:, MISC_BETA:MISC_BETA + N_HEADS_B].set(bb)
    misc = misc.at[:, MISC_A:MISC_A + N_HEADS_B].set(ab)
    xc = jnp.concatenate([qb, kb_, vb], axis=1)
    return [m.astype(BF16) for m in (qa_pad, ka, va, v_pads[0], v_pads[1], qi, misc, xc, zb, gate_a, gate_b)]


def _pad_branch_a(w_branch_a):
    d = w_branch_a.shape[1]
    out = jnp.zeros((WIDTH_A_PAD, d), w_branch_a.dtype)
    for hq in range(N_HEADS_A):
        out = out.at[hq * LANES:hq * LANES + HEAD_DIM_A].set(w_branch_a[hq * HEAD_DIM_A:(hq + 1) * HEAD_DIM_A])
    return out


def _lane_pad(vec, start):
    return jnp.zeros((1, MISC_W), F32).at[0, start:start + vec.shape[0]].set(vec)


def kernel(x_prompt, x_sample, cache_attn_k, cache_attn_v, cache_idx_k, state_conv, state_delta,
           ffn1_norm, ffn1_w_gate, ffn1_w_up, ffn1_w_down, mix_norm, w_in, conv_w, a_log, dt_bias,
           delta_norm, w_branch_a, w_branch_b, w_out, ffn2_norm, ffn2_w_gate, ffn2_w_up, ffn2_w_down,
           final_norm):
    depth = ffn1_norm.shape[0]
    xp, xs = x_prompt, x_sample
    st_p, st_s = [], []
    row = lambda a: a.reshape(1, -1)
    for l in range(depth):
        w = {
            "ffn1_norm": row(ffn1_norm[l]), "ffn1_w_gate": ffn1_w_gate[l].astype(BF16),
            "ffn1_w_up": ffn1_w_up[l].astype(BF16), "ffn1_w_down": ffn1_w_down[l].astype(BF16),
            "mix_norm": row(mix_norm[l]), "w_in_groups": _split_w_in(w_in[l]), "conv_w": conv_w[l],
            "alog_pad": _lane_pad(a_log[l], MISC_A), "dtb_pad": _lane_pad(dt_bias[l], MISC_A),
            "delta_norm": row(delta_norm[l]),
            "w_branch_a": _pad_branch_a(w_branch_a[l]).astype(BF16), "w_branch_b": w_branch_b[l].astype(BF16),
            "w_out": w_out[l].astype(BF16),
            "ffn2_norm": row(ffn2_norm[l]), "ffn2_w_gate": ffn2_w_gate[l].astype(BF16),
            "ffn2_w_up": ffn2_w_up[l].astype(BF16), "ffn2_w_down": ffn2_w_down[l].astype(BF16),
        }
        last = l == depth - 1
        fin = row(final_norm)
        xp, sp = _layer(xp, w, None, fin, last)
        xs, ss = _layer(xs, w, (cache_attn_k[l], cache_attn_v[l], cache_idx_k[l], state_conv[l], state_delta[l]),
                        fin, last)
        st_p.append(sp)
        st_s.append(ss)

    def stacked(states, i):
        return jnp.stack([s[i] for s in states], axis=0)

    return (xp, xs,
            stacked(st_p, 0), stacked(st_p, 1), stacked(st_p, 2), stacked(st_p, 3), stacked(st_p, 4),
            stacked(st_s, 0), stacked(st_s, 1), stacked(st_s, 2), stacked(st_s, 3), stacked(st_s, 4))
```

```python
import functools
import math

import jax
import jax.numpy as jnp
from jax import lax
from jax.experimental import pallas as pl
from jax.experimental.pallas import tpu as pltpu

F32 = jnp.float32
BF16 = jnp.bfloat16
I32 = jnp.int32

CHUNK = 64
N_HEADS_A = 8
N_KV_A = 2
GROUP_A = N_HEADS_A // N_KV_A
HEAD_DIM_A = 64
N_IDX_HEADS = 4
IDX_DIM = 64
TOPK_MAX = 256
N_HEADS_B = 4
DK_B = 128
DV_B = 128
CONV_W = 4
EPS = 1e-6

WIDTH_A = N_HEADS_A * HEAD_DIM_A
WIDTH_KV_A = N_KV_A * HEAD_DIM_A
WIDTH_IDX_Q = N_IDX_HEADS * IDX_DIM
WIDTH_QK_B = N_HEADS_B * DK_B
WIDTH_V_B = N_HEADS_B * DV_B
CONV_CH = 2 * WIDTH_QK_B + WIDTH_V_B

MISC_W = 128
MISC_WI = IDX_DIM
MISC_BETA = MISC_WI + N_IDX_HEADS
MISC_A = 72

V7X_VMEM_LIMIT = 56 * 1024 * 1024
LANES = 128
WIDTH_A_PAD = N_HEADS_A * LANES
INT_MIN = -2 ** 31
INT_MAX = 2 ** 31 - 1
NEG_BIG = -1e30
LOG2E = math.log2(math.e)

HIGHEST = lax.Precision.HIGHEST


def _sigmoid(x):
    return 1.0 / (1.0 + jnp.exp(-x))


def _softplus(x):
    return jnp.maximum(x, 0.0) + jnp.log(1.0 + jnp.exp(-jnp.abs(x)))


def _rms(x, g):
    return x * lax.rsqrt(jnp.mean(x * x, axis=-1, keepdims=True) + EPS) * g


def _dot(a, b, precision=None):
    return jnp.dot(a, b, preferred_element_type=F32, precision=precision)


def _dot_nt(a, b, precision=None):
    return lax.dot_general(a, b, (((1,), (1,)), ((), ())), preferred_element_type=F32, precision=precision)


def _dot_tn(a, b, precision=None):
    return lax.dot_general(a, b, (((0,), (0,)), ((), ())), preferred_element_type=F32, precision=precision)


def _mm(a, b):
    return _dot(a.astype(BF16), b.astype(BF16))


def _mm_nt(a, b):
    return _dot_nt(a.astype(BF16), b.astype(BF16))


def _mm_tn(a, b):
    return _dot_tn(a.astype(BF16), b.astype(BF16))


def _resident(shape):
    nd = len(shape)
    return pl.BlockSpec(shape, lambda *_: (0,) * nd, pipeline_mode=pl.Buffered(1))


def _params(n_axes):
    return pltpu.CompilerParams(dimension_semantics=("arbitrary",) * n_axes, vmem_limit_bytes=V7X_VMEM_LIMIT)


def _swiglu_residual(x, g_ref, wg_ref, wu_ref, wd_ref, n_fc):
    hb = _rms(x, g_ref[...]).astype(BF16)
    fc = wg_ref.shape[1] // n_fc
    acc = None
    for c in range(n_fc):
        sl = slice(c * fc, (c + 1) * fc)
        gate = _dot(hb, wg_ref[:, sl])
        up = _dot(hb, wu_ref[:, sl])
        a = (gate * _sigmoid(gate) * up).astype(BF16)
        part = _dot(a, wd_ref[sl, :])
        acc = part if acc is None else acc + part
    return x + 0.5 * acc


def _ffn_kernel(x_ref, g_ref, wg_ref, wu_ref, wd_ref, o_ref, *, n_fc):
    o_ref[...] = _swiglu_residual(x_ref[...], g_ref, wg_ref, wu_ref, wd_ref, n_fc)


def _ffn_call(x, g, wg, wu, wd, tm, n_fc):
    n, d = x.shape
    f = wg.shape[1]
    return pl.pallas_call(
        functools.partial(_ffn_kernel, n_fc=n_fc),
        out_shape=jax.ShapeDtypeStruct((n, d), F32),
        grid=(n // tm,),
        in_specs=[pl.BlockSpec((tm, d), lambda i: (i, 0)), _resident((1, d)),
                  _resident((d, f)), _resident((d, f)), _resident((f, d))],
        out_specs=pl.BlockSpec((tm, d), lambda i: (i, 0)),
        compiler_params=_params(1),
        name="ffn1",
    )(x, g, wg, wu, wd)


def _out_kernel(x_ref, oa_ref, ob_ref, ga_ref, gb_ref, wa_ref, wb_ref, wo_ref,
                g_ref, wg_ref, wu_ref, wd_ref, gf_ref, o_ref, *, n_fc, final):
    ya = _dot(oa_ref[...], wa_ref[...])
    yb = _dot(ob_ref[...], wb_ref[...])
    merged = _sigmoid(ga_ref[...]) * ya + _sigmoid(gb_ref[...]) * yb
    x2 = x_ref[...] + _dot(merged.astype(BF16), wo_ref[...])
    x3 = _swiglu_residual(x2, g_ref, wg_ref, wu_ref, wd_ref, n_fc)
    o_ref[...] = _rms(x3, gf_ref[...]) if final else x3


def _out_call(x1, oa, ob, ga, gb, wa, wb, wo, g, wg, wu, wd, gf, final, tm, n_fc):
    n, d = x1.shape
    f = wg.shape[1]
    tok = lambda w: pl.BlockSpec((tm, w), lambda i: (i, 0))
    return pl.pallas_call(
        functools.partial(_out_kernel, n_fc=n_fc, final=final),
        out_shape=jax.ShapeDtypeStruct((n, d), F32),
        grid=(n // tm,),
        in_specs=[tok(d), tok(WIDTH_A_PAD), tok(WIDTH_V_B), tok(d), tok(d),
                  _resident((WIDTH_A_PAD, d)), _resident((WIDTH_V_B, d)), _resident((d, d)),
                  _resident((1, d)), _resident((d, f)), _resident((d, f)), _resident((f, d)),
                  _resident((1, d))],
        out_specs=tok(d),
        compiler_params=_params(1),
        name="merge_ffn2",
    )(x1, oa, ob, ga, gb, wa, wb, wo, g, wg, wu, wd, gf)


def _proj_kernel(x_ref, g_ref, wqa_ref, wk_ref, wv_ref, wv0_ref, wv1_ref, wqi_ref, wmisc_ref, wxc_ref, wz_ref,
                 wga_ref, wgb_ref,
                 qa_ref, k_ref, v_ref, kb_ref, v0_ref, v1_ref, qi_ref, ki_ref, kib_ref, misc_ref, xc_ref, z_ref,
                 ga_ref, gb_ref):
    hb = _rms(x_ref[...], g_ref[...]).astype(BF16)
    qa_ref[...] = (_dot(hb, wqa_ref[...]) * (HEAD_DIM_A ** -0.5 * LOG2E)).astype(BF16)
    k = _dot(hb, wk_ref[...])
    k_ref[...] = k
    kb_ref[...] = k.astype(BF16)
    v_ref[...] = _dot(hb, wv_ref[...])
    ones_lane = (lax.broadcasted_iota(I32, (1, LANES), 1) == HEAD_DIM_A).astype(F32)
    v0_ref[...] = (_dot(hb, wv0_ref[...]) + ones_lane).astype(BF16)
    v1_ref[...] = (_dot(hb, wv1_ref[...]) + ones_lane).astype(BF16)
    qi_ref[...] = _dot(hb, wqi_ref[...])
    misc = _dot(hb, wmisc_ref[...])
    misc_ref[...] = misc
    ki = misc[:, :IDX_DIM]
    ki_ref[...] = ki
    kib_ref[...] = ki.astype(BF16)
    xc_ref[...] = _dot(hb, wxc_ref[...])
    z_ref[...] = _dot(hb, wz_ref[...])
    ga_ref[...] = _dot(hb, wga_ref[...])
    gb_ref[...] = _dot(hb, wgb_ref[...])


def _proj_call(x1, g, ws, tm):
    n, d = x1.shape
    widths = [(WIDTH_A_PAD, BF16), (WIDTH_KV_A, F32), (WIDTH_KV_A, F32), (WIDTH_KV_A, BF16),
              (LANES, BF16), (LANES, BF16),
              (WIDTH_IDX_Q, F32), (IDX_DIM, F32), (IDX_DIM, BF16), (MISC_W, F32), (CONV_CH, F32),
              (WIDTH_V_B, F32), (d, F32), (d, F32)]
    tok = lambda w: pl.BlockSpec((tm, w), lambda i: (i, 0))
    return pl.pallas_call(
        _proj_kernel,
        out_shape=[jax.ShapeDtypeStruct((n, w), dt) for w, dt in widths],
        grid=(n // tm,),
        in_specs=[tok(d), _resident((1, d))] + [_resident(w.shape) for w in ws],
        out_specs=[tok(w) for w, _ in widths],
        compiler_params=_params(1),
        name="in_proj",
    )(x1, g, *ws)


def _unit_lower_inverse(a, nilpotent):
    size = a.shape[0]
    n = -a
    eye = (lax.broadcasted_iota(I32, (size, size), 0) == lax.broadcasted_iota(I32, (size, size), 1)).astype(F32)
    t = eye + n
    p = n
    span = 2
    while span < nilpotent:
        p = _mm(p, p)
        t = t + _mm(t, p)
        span *= 2
    return t


def _gdn_kernel(xc_ref, z_ref, misc_ref, conv0_ref, s0_ref, convw_ref, alog_ref, dtb_ref, ng_ref,
                ob_ref, convout_ref, sout_ref, xbuf, act, s_scr, *, c, cpt):
    t = pl.program_id(1)
    tt = c * cpt
    pad = 8

    @pl.when(t == 0)
    def _():
        xbuf[pad - (CONV_W - 1):pad, :] = conv0_ref[...]
        s_scr[...] = s0_ref[...]

    xbuf[pad:pad + tt, :] = xc_ref[...]
    yc = xbuf[pad - 3:pad - 3 + tt, :] * convw_ref[0:1, :]
    for j in range(1, CONV_W):
        yc = yc + xbuf[pad - 3 + j:pad - 3 + j + tt, :] * convw_ref[j:j + 1, :]
    act[...] = yc * _sigmoid(yc)
    convout_ref[...] = xbuf[pad + tt - (CONV_W - 1):pad + tt, :]
    xbuf[0:pad, :] = xbuf[tt:tt + pad, :]

    m = misc_ref[...]
    beta_all = _sigmoid(m)
    g_all = -jnp.exp(alog_ref[...]) * _softplus(m + dtb_ref[...])
    ri = lax.broadcasted_iota(I32, (tt, tt), 0)
    ci = lax.broadcasted_iota(I32, (tt, tt), 1)
    same_chunk_prefix = ((ri // c) == (ci // c)) & (ci <= ri)
    gc_all = _dot(same_chunk_prefix.astype(F32), g_all, HIGHEST)
    pick = (lax.broadcasted_iota(I32, (8, MISC_W), 1) == lax.broadcasted_iota(I32, (8, MISC_W), 0) + MISC_A)
    gc_rows = _dot_nt(pick.astype(F32), gc_all, HIGHEST)

    nh = N_HEADS_B
    n = nh * c
    ri = lax.broadcasted_iota(I32, (n, n), 0)
    ci = lax.broadcasted_iota(I32, (n, n), 1)
    same_head = (ri // c) == (ci // c)
    tri = same_head & (ci <= ri)
    strict = same_head & (ci < ri)

    for ch in range(cpt):
        rs = slice(ch * c, (ch + 1) * c)
        stack = lambda ref, base, w: jnp.concatenate([ref[rs, base + h * w:base + (h + 1) * w] for h in range(nh)], axis=0)
        col = lambda arr, lane0: jnp.concatenate([arr[rs, lane0 + h:lane0 + h + 1] for h in range(nh)], axis=0)
        q = stack(act, 0, DK_B)
        k = stack(act, WIDTH_QK_B, DK_B)
        v = stack(act, 2 * WIDTH_QK_B, DV_B)
        q = q * lax.rsqrt(jnp.sum(q * q, axis=-1, keepdims=True) + EPS) * (DK_B ** -0.5)
        k = k * lax.rsqrt(jnp.sum(k * k, axis=-1, keepdims=True) + EPS)
        b = col(beta_all, MISC_BETA)
        gc = col(gc_all, MISC_A)
        gc_row = jnp.concatenate([gc_rows[h:h + 1, rs] for h in range(nh)], axis=1)
        g_last = jnp.concatenate([jnp.broadcast_to(gc[(h + 1) * c - 1:(h + 1) * c, :], (c, 1))
                                  for h in range(nh)], axis=0)
        decay = jnp.exp(jnp.where(tri, gc - gc_row, -jnp.inf))
        eg = jnp.exp(gc)
        kb = k * b
        a = jnp.where(strict, _mm_nt(kb, k) * decay, 0.0)
        tinv = _unit_lower_inverse(a, c)
        sol = _mm(tinv, jnp.concatenate([v * b, kb * eg], axis=1))
        u = sol[:, :DV_B]
        w = sol[:, DV_B:]
        attn = jnp.where(tri, _mm_nt(q, k) * decay, 0.0)
        qe = q * eg
        kd = k * jnp.exp(g_last - gc)
        v_new, o_state = [], []
        for h in range(nh):
            hs = slice(h * c, (h + 1) * c)
            s = s_scr[h]
            ws = _mm(jnp.concatenate([w[hs], qe[hs]], axis=0), s)
            vn = u[hs] - ws[:c]
            v_new.append(vn)
            o_state.append(ws[c:])
            s_scr[h] = s * jnp.exp(g_last[hs][0:1, :]) + _mm_tn(kd[hs], vn)
        o = jnp.concatenate(o_state, axis=0) + _mm(attn, jnp.concatenate(v_new, axis=0))
        on = o * lax.rsqrt(jnp.mean(o * o, axis=-1, keepdims=True) + EPS) * ng_ref[...]
        zz = stack(z_ref, 0, DV_B)
        gated = (on * (zz * _sigmoid(zz))).astype(ob_ref.dtype)
        for h in range(nh):
            ob_ref[rs, h * DV_B:(h + 1) * DV_B] = gated[h * c:(h + 1) * c]

    @pl.when(t == pl.num_programs(1) - 1)
    def _():
        sout_ref[...] = s_scr[...]


def _gdn_call(xc, z, misc, conv0, s0, convw, alog_pad, dtb_pad, ng, c, cpt):
    b, t, _ = xc.shape
    tt = c * cpt
    tok = lambda w: pl.BlockSpec((None, tt, w), lambda i, j: (i, j, 0))
    per_b = lambda shp: pl.BlockSpec((None,) + shp, lambda i, j: (i,) + (0,) * len(shp))
    return pl.pallas_call(
        functools.partial(_gdn_kernel, c=c, cpt=cpt),
        out_shape=[jax.ShapeDtypeStruct((b, t, WIDTH_V_B), BF16),
                   jax.ShapeDtypeStruct((b, CONV_W - 1, CONV_CH), F32),
                   jax.ShapeDtypeStruct((b, N_HEADS_B, DK_B, DV_B), F32)],
        grid=(b, t // tt),
        in_specs=[tok(CONV_CH), tok(WIDTH_V_B), tok(MISC_W), per_b((CONV_W - 1, CONV_CH)),
                  per_b((N_HEADS_B, DK_B, DV_B)),
                  pl.BlockSpec((CONV_W, CONV_CH), lambda i, j: (0, 0)),
                  pl.BlockSpec((1, MISC_W), lambda i, j: (0, 0)),
                  pl.BlockSpec((1, MISC_W), lambda i, j: (0, 0)),
                  pl.BlockSpec((1, DV_B), lambda i, j: (0, 0))],
        out_specs=[tok(WIDTH_V_B), per_b((CONV_W - 1, CONV_CH)), per_b((N_HEADS_B, DK_B, DV_B))],
        scratch_shapes=[pltpu.VMEM((tt + 8, CONV_CH), F32), pltpu.VMEM((tt, CONV_CH), F32),
                        pltpu.VMEM((N_HEADS_B, DK_B, DV_B), F32)],
        compiler_params=_params(2),
        name="gated_deltanet",
    )(xc, z, misc, conv0, s0, convw, alog_pad, dtb_pad, ng)


def _score_key(score):
    bits = pltpu.bitcast(score, I32)
    mag = bits & INT_MAX
    return jnp.where(bits < 0, -mag, mag)


def _key_score(key):
    bits = jnp.where(key < 0, (-key) | INT_MIN, key)
    return pltpu.bitcast(bits, F32)


def _dsa_kernel(qa_ref, qi_ref, misc_ref, k_ref, v0_ref, v1_ref, ki_ref, o_ref,
                sc_scr, m_scr, acc_scr, alpha_scr, s_scr, *, tq, kt, past, n_keys, n_sel):
    q0 = pl.program_id(1) * tq
    row = lax.broadcasted_iota(I32, (tq, 1), 0)
    pos = past + q0 + row

    def key_limit(p):
        return jnp.minimum((lax.shift_right_logical(p, 6) + 1) * CHUNK, n_keys)

    limit = key_limit(pos)
    nkt = lax.div(key_limit(past + q0 + tq - 1) + kt - 1, kt)
    n_full = lax.div(key_limit(past + q0), kt)
    kkf = jnp.minimum(n_sel, limit).astype(F32)

    misc = misc_ref[...]
    qi = (qi_ref[...] * (IDX_DIM ** -0.5)).astype(BF16)
    qi_h = [qi[:, h * IDX_DIM:(h + 1) * IDX_DIM] for h in range(N_IDX_HEADS)]
    wis = [jnp.broadcast_to(misc[:, MISC_WI + h:MISC_WI + h + 1] * (N_IDX_HEADS ** -0.5), (tq, kt))
           for h in range(N_IDX_HEADS)]

    def score_tile(j, masked):
        ki = ki_ref[pl.ds(pl.multiple_of(j * kt, kt), kt), :]
        score = None
        for h in range(N_IDX_HEADS):
            term = jnp.maximum(_dot_nt(qi_h[h], ki), 0.0) * wis[h]
            score = term if score is None else score + term
        if masked:
            lane = lax.broadcasted_iota(I32, (tq, kt), 1)
            score = jnp.where(lane + j * kt < limit, score, -jnp.inf)
        sc_scr[j] = score

    def full_tile_pairs(p, carry):
        score_tile(2 * p, False)
        score_tile(2 * p + 1, False)
        return carry

    def edge_tiles(j, carry):
        score_tile(j, True)
        return carry

    n_pairs = lax.div(n_full, 2)
    lax.fori_loop(0, n_pairs, full_tile_pairs, 0)
    lax.fori_loop(2 * n_pairs, nkt, edge_tiles, 0)

    def top2(j, carry):
        m1, m2 = carry
        x = sc_scr[j]
        for s in range(kt // LANES):
            xs = x[:, s * LANES:(s + 1) * LANES]
            m2 = jnp.maximum(m2, jnp.minimum(m1, xs))
            m1 = jnp.maximum(m1, xs)
        return m1, m2

    neg_inf = jnp.full((tq, LANES), -jnp.inf, F32)
    m1, m2 = lax.fori_loop(0, nkt, top2, (neg_inf, neg_inf))
    row_max = jnp.max(m1, axis=1, keepdims=True)
    floor = jnp.min(m2, axis=1, keepdims=True) if n_sel <= 2 * LANES else jnp.full((tq, 1), -jnp.inf, F32)
    lo0 = _score_key(floor)
    hi0 = _score_key(row_max) + 1

    def count_ge(thr):
        def body(j, acc):
            c = jnp.where(sc_scr[j] >= thr, 1.0, 0.0)
            part = c[:, 0:LANES]
            for s in range(1, kt // LANES):
                part = part + c[:, s * LANES:(s + 1) * LANES]
            return acc + part
        acc = lax.fori_loop(0, nkt, body, jnp.zeros((tq, LANES), F32))
        return jnp.sum(acc, axis=1, keepdims=True)

    def n_open(lo, hi):
        return jnp.max(jnp.where(hi != lo + 1, 1.0, 0.0))

    def bis_cond(st):
        return st[3] > 0.5

    def bis_body(st):
        lo, hi, chi, _ = st
        mid = (lo >> 1) + (hi >> 1) + (lo & hi & 1)
        mid = jnp.where(lo < 0, jnp.where(hi > 0, 0, mid), jnp.where(lo == 0, jnp.where(hi > 1, 1, mid), mid))
        c = count_ge(_key_score(mid))
        ge = c >= kkf
        exact = c == kkf
        lo_n = jnp.where(ge, mid, lo)
        hi_n = jnp.where(exact, mid + 1, jnp.where(ge, hi, mid))
        chi_n = jnp.where(exact, NEG_BIG, jnp.where(ge, chi, c))
        return lo_n, hi_n, chi_n, n_open(lo_n, hi_n)

    tau_key, _, cnt_gt, _ = lax.while_loop(bis_cond, bis_body,
                                           (lo0, hi0, jnp.zeros((tq, 1), F32), n_open(lo0, hi0)))
    tau = _key_score(tau_key)
    need = kkf - cnt_gt

    ur = lax.broadcasted_iota(I32, (kt, kt), 0)
    uc = lax.broadcasted_iota(I32, (kt, kt), 1)
    before = (ur < uc).astype(BF16)
    m_scr[...] = jnp.full(m_scr.shape, NEG_BIG, F32)
    acc_scr[...] = jnp.zeros(acc_scr.shape, F32)
    v_refs = (v0_ref, v1_ref)

    def attn_tile(j, tie_seen):
        x = sc_scr[j]
        eq = x == tau
        eqf = jnp.where(eq, 1.0, 0.0)
        rank = tie_seen + _dot(eqf.astype(BF16), before)
        sel = (x > tau) | (eq & (rank < need))
        bias = jnp.where(sel, 0.0, NEG_BIG)
        start = pl.multiple_of(j * kt, kt)
        kk = k_ref[pl.ds(start, kt), :]
        for hq in range(N_HEADS_A):
            s = _dot_nt(qa_ref[:, hq * LANES:(hq + 1) * LANES], kk) + bias
            s_scr[hq] = s
            m_old = m_scr[hq]
            m_new = jnp.maximum(m_old, jnp.max(s, axis=1, keepdims=True))
            alpha_scr[hq] = jnp.exp2(m_old - m_new)
            m_scr[hq] = m_new
        for hq in range(N_HEADS_A):
            vv = v_refs[hq // GROUP_A][pl.ds(start, kt), :]
            p = jnp.exp2(s_scr[hq] - jnp.concatenate([m_scr[hq]] * (kt // LANES), axis=1))
            acc_scr[hq] = alpha_scr[hq] * acc_scr[hq] + _dot(p.astype(BF16), vv)
        return tie_seen + jnp.sum(eqf, axis=1, keepdims=True)

    lax.fori_loop(0, nkt, attn_tile, jnp.zeros((tq, 1), F32))

    for hq in range(N_HEADS_A):
        acc = acc_scr[hq]
        o_ref[:, hq * LANES:(hq + 1) * LANES] = (acc / acc[:, HEAD_DIM_A:HEAD_DIM_A + 1]).astype(o_ref.dtype)


def _dsa_call(qa, qi, misc, kb, v0, v1, kib, tq, kt, past, n_keys):
    b, t, _ = qa.shape
    lp = kb.shape[1]
    n_sel = min(TOPK_MAX, n_keys // 4)
    qtok = lambda w: pl.BlockSpec((None, tq, w), lambda i, j: (i, j, 0))
    keys = lambda w: pl.BlockSpec((None, lp, w), lambda i, j: (i, 0, 0))
    return pl.pallas_call(
        functools.partial(_dsa_kernel, tq=tq, kt=kt, past=past, n_keys=n_keys, n_sel=n_sel),
        out_shape=jax.ShapeDtypeStruct((b, t, WIDTH_A_PAD), BF16),
        grid=(b, t // tq),
        in_specs=[qtok(WIDTH_A_PAD), qtok(WIDTH_IDX_Q), qtok(MISC_W),
                  keys(WIDTH_KV_A), keys(LANES), keys(LANES), keys(IDX_DIM)],
        out_specs=qtok(WIDTH_A_PAD),
        scratch_shapes=[pltpu.VMEM((lp // kt, tq, kt), F32),
                        pltpu.VMEM((N_HEADS_A, tq, LANES), F32),
                        pltpu.VMEM((N_HEADS_A, tq, LANES), F32),
                        pltpu.VMEM((N_HEADS_A, tq, LANES), F32),
                        pltpu.VMEM((N_HEADS_A, tq, kt), F32)],
        compiler_params=_params(2),
        name="dsa",
    )(qa, qi, misc, kb, v0, v1, kib)


def _pick_tile(n, pref):
    t = min(n, pref)
    while n % t:
        t //= 2
    return t


def _pad_keys(a, lp):
    return jnp.pad(a, ((0, 0), (0, lp - a.shape[1]), (0, 0)))


def _value_with_ones(v_head):
    b, l, _ = v_head.shape
    tail = jnp.zeros((b, l, LANES - HEAD_DIM_A), v_head.dtype).at[:, :, 0].set(1.0)
    return jnp.concatenate([v_head, tail], axis=-1).astype(BF16)


def _layer(x, w, past, final_norm, final):
    b, t, d = x.shape
    n = b * t
    tm = _pick_tile(n, 256)
    x_flat = x.reshape(n, d)
    x1 = _ffn_call(x_flat, w["ffn1_norm"], w["ffn1_w_gate"], w["ffn1_w_up"], w["ffn1_w_down"], tm, 2)
    (qa, k, v, kb, v0, v1, qi, ki, kib, misc, xc, z, ga, gb) = _proj_call(x1, w["mix_norm"], w["w_in_groups"], tm)
    per_batch = lambda a: a.reshape(b, t, a.shape[-1])

    if past is None:
        conv0 = jnp.zeros((b, CONV_W - 1, CONV_CH), F32)
        s0 = jnp.zeros((b, N_HEADS_B, DK_B, DV_B), F32)
        chunk, past_len = CHUNK, 0
        kb_all, v0_all, v1_all, kib_all = per_batch(kb), per_batch(v0), per_batch(v1), per_batch(kib)
    else:
        ck, cv, cki, conv0, s0 = past
        chunk, past_len = t, ck.shape[1]
        kb_all = jnp.concatenate([ck.reshape(b, past_len, WIDTH_KV_A).astype(BF16), per_batch(kb)], axis=1)
        v0_all = jnp.concatenate([_value_with_ones(cv[:, :, 0, :]), per_batch(v0)], axis=1)
        v1_all = jnp.concatenate([_value_with_ones(cv[:, :, 1, :]), per_batch(v1)], axis=1)
        kib_all = jnp.concatenate([cki.astype(BF16), per_batch(kib)], axis=1)
    n_keys = past_len + t
    kt = 512
    lp = -(-n_keys // kt) * kt
    if lp != n_keys:
        kb_all, v0_all, v1_all, kib_all = [_pad_keys(a, lp) for a in (kb_all, v0_all, v1_all, kib_all)]

    cpt = max(1, _pick_tile(t, 128) // chunk)
    ob, conv_new, s_new = _gdn_call(per_batch(xc), per_batch(z), per_batch(misc), conv0, s0, w["conv_w"],
                                    w["alog_pad"], w["dtb_pad"], w["delta_norm"], chunk, cpt)
    tq = _pick_tile(t, 256)
    oa = _dsa_call(per_batch(qa), per_batch(qi), per_batch(misc), kb_all, v0_all, v1_all, kib_all,
                   tq, kt, past_len, n_keys)

    y = _out_call(x1, oa.reshape(n, WIDTH_A_PAD), ob.reshape(n, WIDTH_V_B), ga, gb,
                  w["w_branch_a"], w["w_branch_b"], w["w_out"],
                  w["ffn2_norm"], w["ffn2_w_gate"], w["ffn2_w_up"], w["ffn2_w_down"], final_norm, final, tm, 2)
    return (y.reshape(b, t, d),
            (k.reshape(b, t, N_KV_A, HEAD_DIM_A), v.reshape(b, t, N_KV_A, HEAD_DIM_A), per_batch(ki),
             conv_new, s_new))


def _split_w_in(w_in):
    d = w_in.shape[0]
    sizes = (WIDTH_A, WIDTH_KV_A, WIDTH_KV_A, WIDTH_IDX_Q, IDX_DIM, N_IDX_HEADS,
             WIDTH_QK_B, WIDTH_QK_B, WIDTH_V_B, WIDTH_V_B, N_HEADS_B, N_HEADS_B, d, d)
    offs = [0]
    for s in sizes:
        offs.append(offs[-1] + s)
    col = lambda i: w_in[:, offs[i]:offs[i + 1]]
    qa, ka, va, qi, ki, wi, qb, kb_, vb, zb, bb, ab, gate_a, gate_b = [col(i) for i in range(len(sizes))]
    qa_pad = jnp.zeros((d, WIDTH_A_PAD), w_in.dtype)
    for hq in range(N_HEADS_A):
        dst = hq * LANES + (hq // GROUP_A) * HEAD_DIM_A
        qa_pad = qa_pad.at[:, dst:dst + HEAD_DIM_A].set(qa[:, hq * HEAD_DIM_A:(hq + 1) * HEAD_DIM_A])
    v_pads = [jnp.zeros((d, LANES), w_in.dtype).at[:, :HEAD_DIM_A].set(va[:, n * HEAD_DIM_A:(n + 1) * HEAD_DIM_A])
              for n in range(N_KV_A)]
    misc = jnp.zeros((d, MISC_W), w_in.dtype)
    misc = misc.at[:, 0:IDX_DIM].set(ki)
    misc = misc.at[:, MISC_WI:MISC_WI + N_IDX_HEADS].set(wi)
    misc = misc.at[:, MISC_BETA:MISC_BETA + N_HEADS_B].set(bb)
    misc = misc.at[:, MISC_A:MISC_A + N_HEADS_B].set(ab)
    xc = jnp.concatenate([qb, kb_, vb], axis=1)
    return [m.astype(BF16) for m in (qa_pad, ka, va, v_pads[0], v_pads[1], qi, misc, xc, zb, gate_a, gate_b)]


def _pad_branch_a(w_branch_a):
    d = w_branch_a.shape[1]
    out = jnp.zeros((WIDTH_A_PAD, d), w_branch_a.dtype)
    for hq in range(N_HEADS_A):
        out = out.at[hq * LANES:hq * LANES + HEAD_DIM_A].set(w_branch_a[hq * HEAD_DIM_A:(hq + 1) * HEAD_DIM_A])
    return out


def _lane_pad(vec, start):
    return jnp.zeros((1, MISC_W), F32).at[0, start:start + vec.shape[0]].set(vec)


def kernel(x_prompt, x_sample, cache_attn_k, cache_attn_v, cache_idx_k, state_conv, state_delta,
           ffn1_norm, ffn1_w_gate, ffn1_w_up, ffn1_w_down, mix_norm, w_in, conv_w, a_log, dt_bias,
           delta_norm, w_branch_a, w_branch_b, w_out, ffn2_norm, ffn2_w_gate, ffn2_w_up, ffn2_w_down,
           final_norm):
    depth = ffn1_norm.shape[0]
    xp, xs = x_prompt, x_sample
    st_p, st_s = [], []
    row = lambda a: a.reshape(1, -1)
    for l in range(depth):
        w = {
            "ffn1_norm": row(ffn1_norm[l]), "ffn1_w_gate": ffn1_w_gate[l].astype(BF16),
            "ffn1_w_up": ffn1_w_up[l].astype(BF16), "ffn1_w_down": ffn1_w_down[l].astype(BF16),
            "mix_norm": row(mix_norm[l]), "w_in_groups": _split_w_in(w_in[l]), "conv_w": conv_w[l],
            "alog_pad": _lane_pad(a_log[l], MISC_A), "dtb_pad": _lane_pad(dt_bias[l], MISC_A),
            "delta_norm": row(delta_norm[l]),
            "w_branch_a": _pad_branch_a(w_branch_a[l]).astype(BF16), "w_branch_b": w_branch_b[l].astype(BF16),
            "w_out": w_out[l].astype(BF16),
            "ffn2_norm": row(ffn2_norm[l]), "ffn2_w_gate": ffn2_w_gate[l].astype(BF16),
            "ffn2_w_up": ffn2_w_up[l].astype(BF16), "ffn2_w_down": ffn2_w_down[l].astype(BF16),
        }
        last = l == depth - 1
        fin = row(final_norm)
        xp, sp = _layer(xp, w, None, fin, last)
        xs, ss = _layer(xs, w, (cache_attn_k[l], cache_attn_v[l], cache_idx_k[l], state_conv[l], state_delta[l]),
                        fin, last)
        st_p.append(sp)
        st_s.append(ss)

    def stacked(states, i):
        return jnp.stack([s[i] for s in states], axis=0)

    return (xp, xs,
            stacked(st_p, 0), stacked(st_p, 1), stacked(st_p, 2), stacked(st_p, 3), stacked(st_p, 4),
            stacked(st_s, 0), stacked(st_s, 1), stacked(st_s, 2), stacked(st_s, 3), stacked(st_s, 4))
```

```python
import functools
import math

import jax
import jax.numpy as jnp
from jax import lax
from jax.experimental import pallas as pl
from jax.experimental.pallas import tpu as pltpu

F32 = jnp.float32
BF16 = jnp.bfloat16
I32 = jnp.int32

CHUNK = 64
N_HEADS_A = 8
N_KV_A = 2
GROUP_A = N_HEADS_A // N_KV_A
HEAD_DIM_A = 64
N_IDX_HEADS = 4
IDX_DIM = 64
TOPK_MAX = 256
N_HEADS_B = 4
DK_B = 128
DV_B = 128
CONV_W = 4
EPS = 1e-6

WIDTH_A = N_HEADS_A * HEAD_DIM_A
WIDTH_KV_A = N_KV_A * HEAD_DIM_A
WIDTH_IDX_Q = N_IDX_HEADS * IDX_DIM
WIDTH_QK_B = N_HEADS_B * DK_B
WIDTH_V_B = N_HEADS_B * DV_B
CONV_CH = 2 * WIDTH_QK_B + WIDTH_V_B

MISC_W = 128
MISC_WI = IDX_DIM
MISC_BETA = MISC_WI + N_IDX_HEADS
MISC_A = 72

V7X_VMEM_LIMIT = 56 * 1024 * 1024
LANES = 128
WIDTH_A_PAD = N_HEADS_A * LANES
INT_MIN = -2 ** 31
INT_MAX = 2 ** 31 - 1
NEG_BIG = -1e30
LOG2E = math.log2(math.e)

HIGHEST = lax.Precision.HIGHEST


def _sigmoid(x):
    return 1.0 / (1.0 + jnp.exp(-x))


def _softplus(x):
    return jnp.maximum(x, 0.0) + jnp.log(1.0 + jnp.exp(-jnp.abs(x)))


def _rms(x, g):
    return x * lax.rsqrt(jnp.mean(x * x, axis=-1, keepdims=True) + EPS) * g


def _dot(a, b, precision=None):
    return jnp.dot(a, b, preferred_element_type=F32, precision=precision)


def _dot_nt(a, b, precision=None):
    return lax.dot_general(a, b, (((1,), (1,)), ((), ())), preferred_element_type=F32, precision=precision)


def _dot_tn(a, b, precision=None):
    return lax.dot_general(a, b, (((0,), (0,)), ((), ())), preferred_element_type=F32, precision=precision)


def _mm(a, b):
    return _dot(a.astype(BF16), b.astype(BF16))


def _mm_nt(a, b):
    return _dot_nt(a.astype(BF16), b.astype(BF16))


def _mm_tn(a, b):
    return _dot_tn(a.astype(BF16), b.astype(BF16))


def _resident(shape):
    nd = len(shape)
    return pl.BlockSpec(shape, lambda *_: (0,) * nd, pipeline_mode=pl.Buffered(1))


def _params(n_axes):
    return pltpu.CompilerParams(dimension_semantics=("arbitrary",) * n_axes, vmem_limit_bytes=V7X_VMEM_LIMIT)


def _swiglu_residual(x, g_ref, wg_ref, wu_ref, wd_ref, n_fc):
    hb = _rms(x, g_ref[...]).astype(BF16)
    fc = wg_ref.shape[1] // n_fc
    acc = None
    for c in range(n_fc):
        sl = slice(c * fc, (c + 1) * fc)
        gate = _dot(hb, wg_ref[:, sl])
        up = _dot(hb, wu_ref[:, sl])
        a = (gate * _sigmoid(gate) * up).astype(BF16)
        part = _dot(a, wd_ref[sl, :])
        acc = part if acc is None else acc + part
    return x + 0.5 * acc


def _ffn_kernel(x_ref, g_ref, wg_ref, wu_ref, wd_ref, o_ref, *, n_fc):
    o_ref[...] = _swiglu_residual(x_ref[...], g_ref, wg_ref, wu_ref, wd_ref, n_fc)


def _ffn_call(x, g, wg, wu, wd, tm, n_fc):
    n, d = x.shape
    f = wg.shape[1]
    return pl.pallas_call(
        functools.partial(_ffn_kernel, n_fc=n_fc),
        out_shape=jax.ShapeDtypeStruct((n, d), F32),
        grid=(n // tm,),
        in_specs=[pl.BlockSpec((tm, d), lambda i: (i, 0)), _resident((1, d)),
                  _resident((d, f)), _resident((d, f)), _resident((f, d))],
        out_specs=pl.BlockSpec((tm, d), lambda i: (i, 0)),
        compiler_params=_params(1),
        name="ffn1",
    )(x, g, wg, wu, wd)


def _out_kernel(x_ref, oa_ref, ob_ref, ga_ref, gb_ref, wa_ref, wb_ref, wo_ref,
                g_ref, wg_ref, wu_ref, wd_ref, gf_ref, o_ref, *, n_fc, final):
    ya = _dot(oa_ref[...], wa_ref[...])
    yb = _dot(ob_ref[...], wb_ref[...])
    merged = _sigmoid(ga_ref[...]) * ya + _sigmoid(gb_ref[...]) * yb
    x2 = x_ref[...] + _dot(merged.astype(BF16), wo_ref[...])
    x3 = _swiglu_residual(x2, g_ref, wg_ref, wu_ref, wd_ref, n_fc)
    o_ref[...] = _rms(x3, gf_ref[...]) if final else x3


def _out_call(x1, oa, ob, ga, gb, wa, wb, wo, g, wg, wu, wd, gf, final, tm, n_fc):
    n, d = x1.shape
    f = wg.shape[1]
    tok = lambda w: pl.BlockSpec((tm, w), lambda i: (i, 0))
    return pl.pallas_call(
        functools.partial(_out_kernel, n_fc=n_fc, final=final),
        out_shape=jax.ShapeDtypeStruct((n, d), F32),
        grid=(n // tm,),
        in_specs=[tok(d), tok(WIDTH_A_PAD), tok(WIDTH_V_B), tok(d), tok(d),
                  _resident((WIDTH_A_PAD, d)), _resident((WIDTH_V_B, d)), _resident((d, d)),
                  _resident((1, d)), _resident((d, f)), _resident((d, f)), _resident((f, d)),
                  _resident((1, d))],
        out_specs=tok(d),
        compiler_params=_params(1),
        name="merge_ffn2",
    )(x1, oa, ob, ga, gb, wa, wb, wo, g, wg, wu, wd, gf)


def _proj_kernel(x_ref, g_ref, wqa_ref, wk_ref, wv_ref, wv0_ref, wv1_ref, wqi_ref, wmisc_ref, wxc_ref, wz_ref,
                 wga_ref, wgb_ref,
                 qa_ref, k_ref, v_ref, kb_ref, v0_ref, v1_ref, qi_ref, ki_ref, kib_ref, misc_ref, xc_ref, z_ref,
                 ga_ref, gb_ref):
    hb = _rms(x_ref[...], g_ref[...]).astype(BF16)
    qa_ref[...] = (_dot(hb, wqa_ref[...]) * (HEAD_DIM_A ** -0.5 * LOG2E)).astype(BF16)
    k = _dot(hb, wk_ref[...])
    k_ref[...] = k
    kb_ref[...] = k.astype(BF16)
    v_ref[...] = _dot(hb, wv_ref[...])
    ones_lane = (lax.broadcasted_iota(I32, (1, LANES), 1) == HEAD_DIM_A).astype(F32)
    v0_ref[...] = (_dot(hb, wv0_ref[...]) + ones_lane).astype(BF16)
    v1_ref[...] = (_dot(hb, wv1_ref[...]) + ones_lane).astype(BF16)
    qi_ref[...] = _dot(hb, wqi_ref[...])
    misc = _dot(hb, wmisc_ref[...])
    misc_ref[...] = misc
    ki = misc[:, :IDX_DIM]
    ki_ref[...] = ki
    kib_ref[...] = ki.astype(BF16)
    xc_ref[...] = _dot(hb, wxc_ref[...])
    z_ref[...] = _dot(hb, wz_ref[...])
    ga_ref[...] = _dot(hb, wga_ref[...])
    gb_ref[...] = _dot(hb, wgb_ref[...])


def _proj_call(x1, g, ws, tm):
    n, d = x1.shape
    widths = [(WIDTH_A_PAD, BF16), (WIDTH_KV_A, F32), (WIDTH_KV_A, F32), (WIDTH_KV_A, BF16),
              (LANES, BF16), (LANES, BF16),
              (WIDTH_IDX_Q, F32), (IDX_DIM, F32), (IDX_DIM, BF16), (MISC_W, F32), (CONV_CH, F32),
              (WIDTH_V_B, F32), (d, F32), (d, F32)]
    tok = lambda w: pl.BlockSpec((tm, w), lambda i: (i, 0))
    return pl.pallas_call(
        _proj_kernel,
        out_shape=[jax.ShapeDtypeStruct((n, w), dt) for w, dt in widths],
        grid=(n // tm,),
        in_specs=[tok(d), _resident((1, d))] + [_resident(w.shape) for w in ws],
        out_specs=[tok(w) for w, _ in widths],
        compiler_params=_params(1),
        name="in_proj",
    )(x1, g, *ws)


def _unit_lower_inverse(a, nilpotent):
    size = a.shape[0]
    n = -a
    eye = (lax.broadcasted_iota(I32, (size, size), 0) == lax.broadcasted_iota(I32, (size, size), 1)).astype(F32)
    t = eye + n
    p = n
    span = 2
    while span < nilpotent:
        p = _mm(p, p)
        t = t + _mm(t, p)
        span *= 2
    return t


def _gdn_kernel(xc_ref, z_ref, misc_ref, conv0_ref, s0_ref, convw_ref, alog_ref, dtb_ref, ng_ref,
                ob_ref, convout_ref, sout_ref, xbuf, act, s_scr, *, c, cpt):
    t = pl.program_id(1)
    tt = c * cpt
    pad = 8

    @pl.when(t == 0)
    def _():
        xbuf[pad - (CONV_W - 1):pad, :] = conv0_ref[...]
        s_scr[...] = s0_ref[...]

    xbuf[pad:pad + tt, :] = xc_ref[...]
    yc = xbuf[pad - 3:pad - 3 + tt, :] * convw_ref[0:1, :]
    for j in range(1, CONV_W):
        yc = yc + xbuf[pad - 3 + j:pad - 3 + j + tt, :] * convw_ref[j:j + 1, :]
    act[...] = yc * _sigmoid(yc)
    convout_ref[...] = xbuf[pad + tt - (CONV_W - 1):pad + tt, :]
    xbuf[0:pad, :] = xbuf[tt:tt + pad, :]

    m = misc_ref[...]
    beta_all = _sigmoid(m)
    g_all = -jnp.exp(alog_ref[...]) * _softplus(m + dtb_ref[...])
    ri = lax.broadcasted_iota(I32, (tt, tt), 0)
    ci = lax.broadcasted_iota(I32, (tt, tt), 1)
    same_chunk_prefix = ((ri // c) == (ci // c)) & (ci <= ri)
    gc_all = _dot(same_chunk_prefix.astype(F32), g_all, HIGHEST)
    pick = (lax.broadcasted_iota(I32, (8, MISC_W), 1) == lax.broadcasted_iota(I32, (8, MISC_W), 0) + MISC_A)
    gc_rows = _dot_nt(pick.astype(F32), gc_all, HIGHEST)

    nh = N_HEADS_B
    n = nh * c
    ri = lax.broadcasted_iota(I32, (n, n), 0)
    ci = lax.broadcasted_iota(I32, (n, n), 1)
    same_head = (ri // c) == (ci // c)
    tri = same_head & (ci <= ri)
    strict = same_head & (ci < ri)

    for ch in range(cpt):
        rs = slice(ch * c, (ch + 1) * c)
        stack = lambda ref, base, w: jnp.concatenate([ref[rs, base + h * w:base + (h + 1) * w] for h in range(nh)], axis=0)
        col = lambda arr, lane0: jnp.concatenate([arr[rs, lane0 + h:lane0 + h + 1] for h in range(nh)], axis=0)
        q = stack(act, 0, DK_B)
        k = stack(act, WIDTH_QK_B, DK_B)
        v = stack(act, 2 * WIDTH_QK_B, DV_B)
        q = q * lax.rsqrt(jnp.sum(q * q, axis=-1, keepdims=True) + EPS) * (DK_B ** -0.5)
        k = k * lax.rsqrt(jnp.sum(k * k, axis=-1, keepdims=True) + EPS)
        b = col(beta_all, MISC_BETA)
        gc = col(gc_all, MISC_A)
        gc_row = jnp.concatenate([gc_rows[h:h + 1, rs] for h in range(nh)], axis=1)
        g_last = jnp.concatenate([jnp.broadcast_to(gc[(h + 1) * c - 1:(h + 1) * c, :], (c, 1))
                                  for h in range(nh)], axis=0)
        decay = jnp.exp(jnp.where(tri, gc - gc_row, -jnp.inf))
        eg = jnp.exp(gc)
        kb = k * b
        a = jnp.where(strict, _mm_nt(kb, k) * decay, 0.0)
        tinv = _unit_lower_inverse(a, c)
        sol = _mm(tinv, jnp.concatenate([v * b, kb * eg], axis=1))
        u = sol[:, :DV_B]
        w = sol[:, DV_B:]
        attn = jnp.where(tri, _mm_nt(q, k) * decay, 0.0)
        qe = q * eg
        kd = k * jnp.exp(g_last - gc)
        v_new, o_state = [], []
        for h in range(nh):
            hs = slice(h * c, (h + 1) * c)
            s = s_scr[h]
            ws = _mm(jnp.concatenate([w[hs], qe[hs]], axis=0), s)
            vn = u[hs] - ws[:c]
            v_new.append(vn)
            o_state.append(ws[c:])
            s_scr[h] = s * jnp.exp(g_last[hs][0:1, :]) + _mm_tn(kd[hs], vn)
        o = jnp.concatenate(o_state, axis=0) + _mm(attn, jnp.concatenate(v_new, axis=0))
        on = o * lax.rsqrt(jnp.mean(o * o, axis=-1, keepdims=True) + EPS) * ng_ref[...]
        zz = stack(z_ref, 0, DV_B)
        gated = (on * (zz * _sigmoid(zz))).astype(ob_ref.dtype)
        for h in range(nh):
            ob_ref[rs, h * DV_B:(h + 1) * DV_B] = gated[h * c:(h + 1) * c]

    @pl.when(t == pl.num_programs(1) - 1)
    def _():
        sout_ref[...] = s_scr[...]


def _gdn_call(xc, z, misc, conv0, s0, convw, alog_pad, dtb_pad, ng, c, cpt):
    b, t, _ = xc.shape
    tt = c * cpt
    tok = lambda w: pl.BlockSpec((None, tt, w), lambda i, j: (i, j, 0))
    per_b = lambda shp: pl.BlockSpec((None,) + shp, lambda i, j: (i,) + (0,) * len(shp))
    return pl.pallas_call(
        functools.partial(_gdn_kernel, c=c, cpt=cpt),
        out_shape=[jax.ShapeDtypeStruct((b, t, WIDTH_V_B), BF16),
                   jax.ShapeDtypeStruct((b, CONV_W - 1, CONV_CH), F32),
                   jax.ShapeDtypeStruct((b, N_HEADS_B, DK_B, DV_B), F32)],
        grid=(b, t // tt),
        in_specs=[tok(CONV_CH), tok(WIDTH_V_B), tok(MISC_W), per_b((CONV_W - 1, CONV_CH)),
                  per_b((N_HEADS_B, DK_B, DV_B)),
                  pl.BlockSpec((CONV_W, CONV_CH), lambda i, j: (0, 0)),
                  pl.BlockSpec((1, MISC_W), lambda i, j: (0, 0)),
                  pl.BlockSpec((1, MISC_W), lambda i, j: (0, 0)),
                  pl.BlockSpec((1, DV_B), lambda i, j: (0, 0))],
        out_specs=[tok(WIDTH_V_B), per_b((CONV_W - 1, CONV_CH)), per_b((N_HEADS_B, DK_B, DV_B))],
        scratch_shapes=[pltpu.VMEM((tt + 8, CONV_CH), F32), pltpu.VMEM((tt, CONV_CH), F32),
                        pltpu.VMEM((N_HEADS_B, DK_B, DV_B), F32)],
        compiler_params=_params(2),
        name="gated_deltanet",
    )(xc, z, misc, conv0, s0, convw, alog_pad, dtb_pad, ng)


def _score_key(score):
    bits = pltpu.bitcast(score, I32)
    mag = bits & INT_MAX
    return jnp.where(bits < 0, -mag, mag)


def _key_score(key):
    bits = jnp.where(key < 0, (-key) | INT_MIN, key)
    return pltpu.bitcast(bits, F32)


def _dsa_kernel(qa_ref, qi_ref, misc_ref, k_ref, v0_ref, v1_ref, ki_ref, o_ref,
                sc_scr, m_scr, acc_scr, alpha_scr, s_scr, *, tq, kt, past, n_keys, n_sel):
    q0 = pl.program_id(1) * tq
    row = lax.broadcasted_iota(I32, (tq, 1), 0)
    pos = past + q0 + row

    def key_limit(p):
        return jnp.minimum((lax.shift_right_logical(p, 6) + 1) * CHUNK, n_keys)

    limit = key_limit(pos)
    nkt = lax.div(key_limit(past + q0 + tq - 1) + kt - 1, kt)
    n_full = lax.div(key_limit(past + q0), kt)
    kkf = jnp.minimum(n_sel, limit).astype(F32)

    misc = misc_ref[...]
    qi = (qi_ref[...] * (IDX_DIM ** -0.5)).astype(BF16)
    qi_h = [qi[:, h * IDX_DIM:(h + 1) * IDX_DIM] for h in range(N_IDX_HEADS)]
    wis = [jnp.broadcast_to(misc[:, MISC_WI + h:MISC_WI + h + 1] * (N_IDX_HEADS ** -0.5), (tq, kt))
           for h in range(N_IDX_HEADS)]

    def score_tile(j, masked):
        ki = ki_ref[pl.ds(pl.multiple_of(j * kt, kt), kt), :]
        score = None
        for h in range(N_IDX_HEADS):
            term = jnp.maximum(_dot_nt(qi_h[h], ki), 0.0) * wis[h]
            score = term if score is None else score + term
        if masked:
            lane = lax.broadcasted_iota(I32, (tq, kt), 1)
            score = jnp.where(lane + j * kt < limit, score, -jnp.inf)
        sc_scr[j] = score

    def full_tile_pairs(p, carry):
        score_tile(2 * p, False)
        score_tile(2 * p + 1, False)
        return carry

    def edge_tiles(j, carry):
        score_tile(j, True)
        return carry

    n_pairs = lax.div(n_full, 2)
    lax.fori_loop(0, n_pairs, full_tile_pairs, 0)
    lax.fori_loop(2 * n_pairs, nkt, edge_tiles, 0)

    def top2(j, carry):
        m1, m2 = carry
        x = sc_scr[j]
        for s in range(kt // LANES):
            xs = x[:, s * LANES:(s + 1) * LANES]
            m2 = jnp.maximum(m2, jnp.minimum(m1, xs))
            m1 = jnp.maximum(m1, xs)
        return m1, m2

    neg_inf = jnp.full((tq, LANES), -jnp.inf, F32)
    m1, m2 = lax.fori_loop(0, nkt, top2, (neg_inf, neg_inf))
    row_max = jnp.max(m1, axis=1, keepdims=True)
    floor = jnp.min(m2, axis=1, keepdims=True) if n_sel <= 2 * LANES else jnp.full((tq, 1), -jnp.inf, F32)
    lo0 = _score_key(floor)
    hi0 = _score_key(row_max) + 1

    def count_ge(thr):
        def body(j, acc):
            c = jnp.where(sc_scr[j] >= thr, 1.0, 0.0)
            part = c[:, 0:LANES]
            for s in range(1, kt // LANES):
                part = part + c[:, s * LANES:(s + 1) * LANES]
            return acc + part
        acc = lax.fori_loop(0, nkt, body, jnp.zeros((tq, LANES), F32))
        return jnp.sum(acc, axis=1, keepdims=True)

    def n_open(lo, hi):
        return jnp.max(jnp.where(hi != lo + 1, 1.0, 0.0))

    def bis_cond(st):
        return st[3] > 0.5

    def bis_body(st):
        lo, hi, chi, _ = st
        mid = (lo >> 1) + (hi >> 1) + (lo & hi & 1)
        mid = jnp.where(lo < 0, jnp.where(hi > 0, 0, mid), jnp.where(lo == 0, jnp.where(hi > 1, 1, mid), mid))
        c = count_ge(_key_score(mid))
        ge = c >= kkf
        exact = c == kkf
        lo_n = jnp.where(ge, mid, lo)
        hi_n = jnp.where(exact, mid + 1, jnp.where(ge, hi, mid))
        chi_n = jnp.where(exact, NEG_BIG, jnp.where(ge, chi, c))
        return lo_n, hi_n, chi_n, n_open(lo_n, hi_n)

    tau_key, _, cnt_gt, _ = lax.while_loop(bis_cond, bis_body,
                                           (lo0, hi0, jnp.zeros((tq, 1), F32), n_open(lo0, hi0)))
    tau = _key_score(tau_key)
    need = kkf - cnt_gt

    ur = lax.broadcasted_iota(I32, (kt, kt), 0)
    uc = lax.broadcasted_iota(I32, (kt, kt), 1)
    before = (ur < uc).astype(BF16)
    m_scr[...] = jnp.full(m_scr.shape, NEG_BIG, F32)
    acc_scr[...] = jnp.zeros(acc_scr.shape, F32)
    v_refs = (v0_ref, v1_ref)

    def attn_tile(j, tie_seen):
        x = sc_scr[j]
        eq = x == tau
        eqf = jnp.where(eq, 1.0, 0.0)
        rank = tie_seen + _dot(eqf.astype(BF16), before)
        sel = (x > tau) | (eq & (rank < need))
        bias = jnp.where(sel, 0.0, NEG_BIG)
        start = pl.multiple_of(j * kt, kt)
        kk = k_ref[pl.ds(start, kt), :]
        for hq in range(N_HEADS_A):
            s = _dot_nt(qa_ref[:, hq * LANES:(hq + 1) * LANES], kk) + bias
            s_scr[hq] = s
            m_old = m_scr[hq]
            m_new = jnp.maximum(m_old, jnp.max(s, axis=1, keepdims=True))
            alpha_scr[hq] = jnp.exp2(m_old - m_new)
            m_scr[hq] = m_new
        for hq in range(N_HEADS_A):
            vv = v_refs[hq // GROUP_A][pl.ds(start, kt), :]
            p = jnp.exp2(s_scr[hq] - jnp.concatenate([m_scr[hq]] * (kt // LANES), axis=1))
            acc_scr[hq] = alpha_scr[hq] * acc_scr[hq] + _dot(p.astype(BF16), vv)
        return tie_seen + jnp.sum(eqf, axis=1, keepdims=True)

    lax.fori_loop(0, nkt, attn_tile, jnp.zeros((tq, 1), F32))

    for hq in range(N_HEADS_A):
        acc = acc_scr[hq]
        o_ref[:, hq * LANES:(hq + 1) * LANES] = (acc / acc[:, HEAD_DIM_A:HEAD_DIM_A + 1]).astype(o_ref.dtype)


def _dsa_call(qa, qi, misc, kb, v0, v1, kib, tq, kt, past, n_keys):
    b, t, _ = qa.shape
    lp = kb.shape[1]
    n_sel = min(TOPK_MAX, n_keys // 4)
    qtok = lambda w: pl.BlockSpec((None, tq, w), lambda i, j: (i, j, 0))
    keys = lambda w: pl.BlockSpec((None, lp, w), lambda i, j: (i, 0, 0))
    return pl.pallas_call(
        functools.partial(_dsa_kernel, tq=tq, kt=kt, past=past, n_keys=n_keys, n_sel=n_sel),
        out_shape=jax.ShapeDtypeStruct((b, t, WIDTH_A_PAD), BF16),
        grid=(b, t // tq),
        in_specs=[qtok(WIDTH_A_PAD), qtok(WIDTH_IDX_Q), qtok(MISC_W),
                  keys(WIDTH_KV_A), keys(LANES), keys(LANES), keys(IDX_DIM)],
        out_specs=qtok(WIDTH_A_PAD),
        scratch_shapes=[pltpu.VMEM((lp // kt, tq, kt), F32),
                        pltpu.VMEM((N_HEADS_A, tq, LANES), F32),
                        pltpu.VMEM((N_HEADS_A, tq, LANES), F32),
                        pltpu.VMEM((N_HEADS_A, tq, LANES), F32),
                        pltpu.VMEM((N_HEADS_A, tq, kt), F32)],
        compiler_params=_params(2),
        name="dsa",
    )(qa, qi, misc, kb, v0, v1, kib)


def _pick_tile(n, pref):
    t = min(n, pref)
    while n % t:
        t //= 2
    return t


def _pad_keys(a, lp):
    return jnp.pad(a, ((0, 0), (0, lp - a.shape[1]), (0, 0)))


def _value_with_ones(v_head):
    b, l, _ = v_head.shape
    tail = jnp.zeros((b, l, LANES - HEAD_DIM_A), v_head.dtype).at[:, :, 0].set(1.0)
    return jnp.concatenate([v_head, tail], axis=-1).astype(BF16)


def _layer(x, w, past, final_norm, final):
    b, t, d = x.shape
    n = b * t
    tm = _pick_tile(n, 256)
    tm_big = _pick_tile(n, 512)
    x_flat = x.reshape(n, d)
    x1 = _ffn_call(x_flat, w["ffn1_norm"], w["ffn1_w_gate"], w["ffn1_w_up"], w["ffn1_w_down"], tm_big, 2)
    (qa, k, v, kb, v0, v1, qi, ki, kib, misc, xc, z, ga, gb) = _proj_call(x1, w["mix_norm"], w["w_in_groups"], tm_big)
    per_batch = lambda a: a.reshape(b, t, a.shape[-1])

    if past is None:
        conv0 = jnp.zeros((b, CONV_W - 1, CONV_CH), F32)
        s0 = jnp.zeros((b, N_HEADS_B, DK_B, DV_B), F32)
        chunk, past_len = CHUNK, 0
        kb_all, v0_all, v1_all, kib_all = per_batch(kb), per_batch(v0), per_batch(v1), per_batch(kib)
    else:
        ck, cv, cki, conv0, s0 = past
        chunk, past_len = t, ck.shape[1]
        kb_all = jnp.concatenate([ck.reshape(b, past_len, WIDTH_KV_A).astype(BF16), per_batch(kb)], axis=1)
        v0_all = jnp.concatenate([_value_with_ones(cv[:, :, 0, :]), per_batch(v0)], axis=1)
        v1_all = jnp.concatenate([_value_with_ones(cv[:, :, 1, :]), per_batch(v1)], axis=1)
        kib_all = jnp.concatenate([cki.astype(BF16), per_batch(kib)], axis=1)
    n_keys = past_len + t
    kt = 512
    lp = -(-n_keys // kt) * kt
    if lp != n_keys:
        kb_all, v0_all, v1_all, kib_all = [_pad_keys(a, lp) for a in (kb_all, v0_all, v1_all, kib_all)]

    cpt = max(1, _pick_tile(t, 256) // chunk)
    ob, conv_new, s_new = _gdn_call(per_batch(xc), per_batch(z), per_batch(misc), conv0, s0, w["conv_w"],
                                    w["alog_pad"], w["dtb_pad"], w["delta_norm"], chunk, cpt)
    tq = _pick_tile(t, 256)
    oa = _dsa_call(per_batch(qa), per_batch(qi), per_batch(misc), kb_all, v0_all, v1_all, kib_all,
                   tq, kt, past_len, n_keys)

    y = _out_call(x1, oa.reshape(n, WIDTH_A_PAD), ob.reshape(n, WIDTH_V_B), ga, gb,
                  w["w_branch_a"], w["w_branch_b"], w["w_out"],
                  w["ffn2_norm"], w["ffn2_w_gate"], w["ffn2_w_up"], w["ffn2_w_down"], final_norm, final, tm, 2)
    return (y.reshape(b, t, d),
            (k.reshape(b, t, N_KV_A, HEAD_DIM_A), v.reshape(b, t, N_KV_A, HEAD_DIM_A), per_batch(ki),
             conv_new, s_new))


def _split_w_in(w_in):
    d = w_in.shape[0]
    sizes = (WIDTH_A, WIDTH_KV_A, WIDTH_KV_A, WIDTH_IDX_Q, IDX_DIM, N_IDX_HEADS,
             WIDTH_QK_B, WIDTH_QK_B, WIDTH_V_B, WIDTH_V_B, N_HEADS_B, N_HEADS_B, d, d)
    offs = [0]
    for s in sizes:
        offs.append(offs[-1] + s)
    col = lambda i: w_in[:, offs[i]:offs[i + 1]]
    qa, ka, va, qi, ki, wi, qb, kb_, vb, zb, bb, ab, gate_a, gate_b = [col(i) for i in range(len(sizes))]
    qa_pad = jnp.zeros((d, WIDTH_A_PAD), w_in.dtype)
    for hq in range(N_HEADS_A):
        dst = hq * LANES + (hq // GROUP_A) * HEAD_DIM_A
        qa_pad = qa_pad.at[:, dst:dst + HEAD_DIM_A].set(qa[:, hq * HEAD_DIM_A:(hq + 1) * HEAD_DIM_A])
    v_pads = [jnp.zeros((d, LANES), w_in.dtype).at[:, :HEAD_DIM_A].set(va[:, n * HEAD_DIM_A:(n + 1) * HEAD_DIM_A])
              for n in range(N_KV_A)]
    misc = jnp.zeros((d, MISC_W), w_in.dtype)
    misc = misc.at[:, 0:IDX_DIM].set(ki)
    misc = misc.at[:, MISC_WI:MISC_WI + N_IDX_HEADS].set(wi)
    misc = misc.at[:, MISC_BETA:MISC_BETA + N_HEADS_B].set(bb)
    misc = misc.at[:, MISC_A:MISC_A + N_HEADS_B].set(ab)
    xc = jnp.concatenate([qb, kb_, vb], axis=1)
    return [m.astype(BF16) for m in (qa_pad, ka, va, v_pads[0], v_pads[1], qi, misc, xc, zb, gate_a, gate_b)]


def _pad_branch_a(w_branch_a):
    d = w_branch_a.shape[1]
    out = jnp.zeros((WIDTH_A_PAD, d), w_branch_a.dtype)
    for hq in range(N_HEADS_A):
        out = out.at[hq * LANES:hq * LANES + HEAD_DIM_A].set(w_branch_a[hq * HEAD_DIM_A:(hq + 1) * HEAD_DIM_A])
    return out


def _lane_pad(vec, start):
    return jnp.zeros((1, MISC_W), F32).at[0, start:start + vec.shape[0]].set(vec)


def kernel(x_prompt, x_sample, cache_attn_k, cache_attn_v, cache_idx_k, state_conv, state_delta,
           ffn1_norm, ffn1_w_gate, ffn1_w_up, ffn1_w_down, mix_norm, w_in, conv_w, a_log, dt_bias,
           delta_norm, w_branch_a, w_branch_b, w_out, ffn2_norm, ffn2_w_gate, ffn2_w_up, ffn2_w_down,
           final_norm):
    depth = ffn1_norm.shape[0]
    xp, xs = x_prompt, x_sample
    st_p, st_s = [], []
    row = lambda a: a.reshape(1, -1)
    for l in range(depth):
        w = {
            "ffn1_norm": row(ffn1_norm[l]), "ffn1_w_gate": ffn1_w_gate[l].astype(BF16),
            "ffn1_w_up": ffn1_w_up[l].astype(BF16), "ffn1_w_down": ffn1_w_down[l].astype(BF16),
            "mix_norm": row(mix_norm[l]), "w_in_groups": _split_w_in(w_in[l]), "conv_w": conv_w[l],
            "alog_pad": _lane_pad(a_log[l], MISC_A), "dtb_pad": _lane_pad(dt_bias[l], MISC_A),
            "delta_norm": row(delta_norm[l]),
            "w_branch_a": _pad_branch_a(w_branch_a[l]).astype(BF16), "w_branch_b": w_branch_b[l].astype(BF16),
            "w_out": w_out[l].astype(BF16),
            "ffn2_norm": row(ffn2_norm[l]), "ffn2_w_gate": ffn2_w_gate[l].astype(BF16),
            "ffn2_w_up": ffn2_w_up[l].astype(BF16), "ffn2_w_down": ffn2_w_down[l].astype(BF16),
        }
        last = l == depth - 1
        fin = row(final_norm)
        xp, sp = _layer(xp, w, None, fin, last)
        xs, ss = _layer(xs, w, (cache_attn_k[l], cache_attn_v[l], cache_idx_k[l], state_conv[l], state_delta[l]),
                        fin, last)
        st_p.append(sp)
        st_s.append(ss)

    def stacked(states, i):
        return jnp.stack([s[i] for s in states], axis=0)

    return (xp, xs,
            stacked(st_p, 0), stacked(st_p, 1), stacked(st_p, 2), stacked(st_p, 3), stacked(st_p, 4),
            stacked(st_s, 0), stacked(st_s, 1), stacked(st_s, 2), stacked(st_s, 3), stacked(st_s, 4))
```

```python
import functools
import math

import jax
import jax.numpy as jnp
from jax import lax
from jax.experimental import pallas as pl
from jax.experimental.pallas import tpu as pltpu

F32 = jnp.float32
BF16 = jnp.bfloat16
I32 = jnp.int32

CHUNK = 64
N_HEADS_A = 8
N_KV_A = 2
GROUP_A = N_HEADS_A // N_KV_A
HEAD_DIM_A = 64
N_IDX_HEADS = 4
IDX_DIM = 64
TOPK_MAX = 256
N_HEADS_B = 4
DK_B = 128
DV_B = 128
CONV_W = 4
EPS = 1e-6

WIDTH_A = N_HEADS_A * HEAD_DIM_A
WIDTH_KV_A = N_KV_A * HEAD_DIM_A
WIDTH_IDX_Q = N_IDX_HEADS * IDX_DIM
WIDTH_QK_B = N_HEADS_B * DK_B
WIDTH_V_B = N_HEADS_B * DV_B
CONV_CH = 2 * WIDTH_QK_B + WIDTH_V_B

MISC_W = 128
MISC_WI = IDX_DIM
MISC_BETA = MISC_WI + N_IDX_HEADS
MISC_A = 72

V7X_VMEM_LIMIT = 56 * 1024 * 1024
LANES = 128
WIDTH_A_PAD = N_HEADS_A * LANES
INT_MIN = -2 ** 31
INT_MAX = 2 ** 31 - 1
NEG_BIG = -1e30
LOG2E = math.log2(math.e)

HIGHEST = lax.Precision.HIGHEST


def _sigmoid(x):
    return 1.0 / (1.0 + jnp.exp(-x))


def _softplus(x):
    return jnp.maximum(x, 0.0) + jnp.log(1.0 + jnp.exp(-jnp.abs(x)))


def _rms(x, g):
    return x * lax.rsqrt(jnp.mean(x * x, axis=-1, keepdims=True) + EPS) * g


def _dot(a, b, precision=None):
    return jnp.dot(a, b, preferred_element_type=F32, precision=precision)


def _dot_nt(a, b, precision=None):
    return lax.dot_general(a, b, (((1,), (1,)), ((), ())), preferred_element_type=F32, precision=precision)


def _dot_tn(a, b, precision=None):
    return lax.dot_general(a, b, (((0,), (0,)), ((), ())), preferred_element_type=F32, precision=precision)


def _mm(a, b):
    return _dot(a.astype(BF16), b.astype(BF16))


def _mm_nt(a, b):
    return _dot_nt(a.astype(BF16), b.astype(BF16))


def _mm_tn(a, b):
    return _dot_tn(a.astype(BF16), b.astype(BF16))


def _resident(shape):
    nd = len(shape)
    return pl.BlockSpec(shape, lambda *_: (0,) * nd, pipeline_mode=pl.Buffered(1))


def _params(n_axes):
    return pltpu.CompilerParams(dimension_semantics=("arbitrary",) * n_axes, vmem_limit_bytes=V7X_VMEM_LIMIT)


def _swiglu_residual(x, g_ref, wg_ref, wu_ref, wd_ref, n_fc):
    hb = _rms(x, g_ref[...]).astype(BF16)
    fc = wg_ref.shape[1] // n_fc
    acc = None
    for c in range(n_fc):
        sl = slice(c * fc, (c + 1) * fc)
        gate = _dot(hb, wg_ref[:, sl])
        up = _dot(hb, wu_ref[:, sl])
        a = (gate * _sigmoid(gate) * up).astype(BF16)
        part = _dot(a, wd_ref[sl, :])
        acc = part if acc is None else acc + part
    return x + 0.5 * acc


def _ffn_kernel(x_ref, g_ref, wg_ref, wu_ref, wd_ref, o_ref, *, n_fc):
    o_ref[...] = _swiglu_residual(x_ref[...], g_ref, wg_ref, wu_ref, wd_ref, n_fc)


def _ffn_call(x, g, wg, wu, wd, tm, n_fc):
    n, d = x.shape
    f = wg.shape[1]
    return pl.pallas_call(
        functools.partial(_ffn_kernel, n_fc=n_fc),
        out_shape=jax.ShapeDtypeStruct((n, d), F32),
        grid=(n // tm,),
        in_specs=[pl.BlockSpec((tm, d), lambda i: (i, 0)), _resident((1, d)),
                  _resident((d, f)), _resident((d, f)), _resident((f, d))],
        out_specs=pl.BlockSpec((tm, d), lambda i: (i, 0)),
        compiler_params=_params(1),
        name="ffn1",
    )(x, g, wg, wu, wd)


def _out_kernel(x_ref, oa_ref, ob_ref, ga_ref, gb_ref, wa_ref, wb_ref, wo_ref,
                g_ref, wg_ref, wu_ref, wd_ref, gf_ref, o_ref, *, n_fc, final):
    ya = _dot(oa_ref[...], wa_ref[...])
    yb = _dot(ob_ref[...], wb_ref[...])
    merged = _sigmoid(ga_ref[...]) * ya + _sigmoid(gb_ref[...]) * yb
    x2 = x_ref[...] + _dot(merged.astype(BF16), wo_ref[...])
    x3 = _swiglu_residual(x2, g_ref, wg_ref, wu_ref, wd_ref, n_fc)
    o_ref[...] = _rms(x3, gf_ref[...]) if final else x3


def _out_call(x1, oa, ob, ga, gb, wa, wb, wo, g, wg, wu, wd, gf, final, tm, n_fc):
    n, d = x1.shape
    f = wg.shape[1]
    tok = lambda w: pl.BlockSpec((tm, w), lambda i: (i, 0))
    return pl.pallas_call(
        functools.partial(_out_kernel, n_fc=n_fc, final=final),
        out_shape=jax.ShapeDtypeStruct((n, d), F32),
        grid=(n // tm,),
        in_specs=[tok(d), tok(WIDTH_A_PAD), tok(WIDTH_V_B), tok(d), tok(d),
                  _resident((WIDTH_A_PAD, d)), _resident((WIDTH_V_B, d)), _resident((d, d)),
                  _resident((1, d)), _resident((d, f)), _resident((d, f)), _resident((f, d)),
                  _resident((1, d))],
        out_specs=tok(d),
        compiler_params=_params(1),
        name="merge_ffn2",
    )(x1, oa, ob, ga, gb, wa, wb, wo, g, wg, wu, wd, gf)


def _proj_kernel(x_ref, g_ref, wqa_ref, wk_ref, wv_ref, wv0_ref, wv1_ref, wqi_ref, wmisc_ref, wxc_ref, wz_ref,
                 wga_ref, wgb_ref,
                 qa_ref, k_ref, v_ref, kb_ref, v0_ref, v1_ref, qi_ref, ki_ref, kib_ref, misc_ref, xc_ref, z_ref,
                 ga_ref, gb_ref):
    hb = _rms(x_ref[...], g_ref[...]).astype(BF16)
    qa_ref[...] = (_dot(hb, wqa_ref[...]) * (HEAD_DIM_A ** -0.5 * LOG2E)).astype(BF16)
    k = _dot(hb, wk_ref[...])
    k_ref[...] = k
    kb_ref[...] = k.astype(BF16)
    v_ref[...] = _dot(hb, wv_ref[...])
    ones_lane = (lax.broadcasted_iota(I32, (1, LANES), 1) == HEAD_DIM_A).astype(F32)
    v0_ref[...] = (_dot(hb, wv0_ref[...]) + ones_lane).astype(BF16)
    v1_ref[...] = (_dot(hb, wv1_ref[...]) + ones_lane).astype(BF16)
    qi_ref[...] = _dot(hb, wqi_ref[...])
    misc = _dot(hb, wmisc_ref[...])
    misc_ref[...] = misc
    ki = misc[:, :IDX_DIM]
    ki_ref[...] = ki
    kib_ref[...] = ki.astype(BF16)
    xc_ref[...] = _dot(hb, wxc_ref[...])
    z_ref[...] = _dot(hb, wz_ref[...])
    ga_ref[...] = _dot(hb, wga_ref[...])
    gb_ref[...] = _dot(hb, wgb_ref[...])


def _proj_call(x1, g, ws, tm):
    n, d = x1.shape
    widths = [(WIDTH_A_PAD, BF16), (WIDTH_KV_A, F32), (WIDTH_KV_A, F32), (WIDTH_KV_A, BF16),
              (LANES, BF16), (LANES, BF16),
              (WIDTH_IDX_Q, F32), (IDX_DIM, F32), (IDX_DIM, BF16), (MISC_W, F32), (CONV_CH, F32),
              (WIDTH_V_B, F32), (d, F32), (d, F32)]
    tok = lambda w: pl.BlockSpec((tm, w), lambda i: (i, 0))
    return pl.pallas_call(
        _proj_kernel,
        out_shape=[jax.ShapeDtypeStruct((n, w), dt) for w, dt in widths],
        grid=(n // tm,),
        in_specs=[tok(d), _resident((1, d))] + [_resident(w.shape) for w in ws],
        out_specs=[tok(w) for w, _ in widths],
        compiler_params=_params(1),
        name="in_proj",
    )(x1, g, *ws)


def _unit_lower_inverse(a, nilpotent):
    size = a.shape[0]
    n = -a
    eye = (lax.broadcasted_iota(I32, (size, size), 0) == lax.broadcasted_iota(I32, (size, size), 1)).astype(F32)
    t = eye + n
    p = n
    span = 2
    while span < nilpotent:
        p = _mm(p, p)
        t = t + _mm(t, p)
        span *= 2
    return t


def _gdn_kernel(xc_ref, z_ref, misc_ref, conv0_ref, s0_ref, convw_ref, alog_ref, dtb_ref, ng_ref,
                ob_ref, convout_ref, sout_ref, xbuf, act, s_scr, *, c, cpt):
    t = pl.program_id(1)
    tt = c * cpt
    pad = 8

    @pl.when(t == 0)
    def _():
        xbuf[pad - (CONV_W - 1):pad, :] = conv0_ref[...]
        s_scr[...] = s0_ref[...]

    xbuf[pad:pad + tt, :] = xc_ref[...]
    yc = xbuf[pad - 3:pad - 3 + tt, :] * convw_ref[0:1, :]
    for j in range(1, CONV_W):
        yc = yc + xbuf[pad - 3 + j:pad - 3 + j + tt, :] * convw_ref[j:j + 1, :]
    act[...] = yc * _sigmoid(yc)
    convout_ref[...] = xbuf[pad + tt - (CONV_W - 1):pad + tt, :]
    xbuf[0:pad, :] = xbuf[tt:tt + pad, :]

    m = misc_ref[...]
    beta_all = _sigmoid(m)
    g_all = -jnp.exp(alog_ref[...]) * _softplus(m + dtb_ref[...])
    ri = lax.broadcasted_iota(I32, (tt, tt), 0)
    ci = lax.broadcasted_iota(I32, (tt, tt), 1)
    same_chunk_prefix = ((ri // c) == (ci // c)) & (ci <= ri)
    gc_all = _dot(same_chunk_prefix.astype(F32), g_all, HIGHEST)
    pick = (lax.broadcasted_iota(I32, (8, MISC_W), 1) == lax.broadcasted_iota(I32, (8, MISC_W), 0) + MISC_A)
    gc_rows = _dot_nt(pick.astype(F32), gc_all, HIGHEST)

    nh = N_HEADS_B
    n = nh * c
    ri = lax.broadcasted_iota(I32, (n, n), 0)
    ci = lax.broadcasted_iota(I32, (n, n), 1)
    same_head = (ri // c) == (ci // c)
    tri = same_head & (ci <= ri)
    strict = same_head & (ci < ri)

    for ch in range(cpt):
        rs = slice(ch * c, (ch + 1) * c)
        stack = lambda ref, base, w: jnp.concatenate([ref[rs, base + h * w:base + (h + 1) * w] for h in range(nh)], axis=0)
        col = lambda arr, lane0: jnp.concatenate([arr[rs, lane0 + h:lane0 + h + 1] for h in range(nh)], axis=0)
        q = stack(act, 0, DK_B)
        k = stack(act, WIDTH_QK_B, DK_B)
        v = stack(act, 2 * WIDTH_QK_B, DV_B)
        q = q * lax.rsqrt(jnp.sum(q * q, axis=-1, keepdims=True) + EPS) * (DK_B ** -0.5)
        k = k * lax.rsqrt(jnp.sum(k * k, axis=-1, keepdims=True) + EPS)
        b = col(beta_all, MISC_BETA)
        gc = col(gc_all, MISC_A)
        gc_row = jnp.concatenate([gc_rows[h:h + 1, rs] for h in range(nh)], axis=1)
        g_last = jnp.concatenate([jnp.broadcast_to(gc[(h + 1) * c - 1:(h + 1) * c, :], (c, 1))
                                  for h in range(nh)], axis=0)
        decay = jnp.exp(jnp.where(tri, gc - gc_row, -jnp.inf))
        eg = jnp.exp(gc)
        kb = k * b
        a = jnp.where(strict, _mm_nt(kb, k) * decay, 0.0)
        tinv = _unit_lower_inverse(a, c)
        sol = _mm(tinv, jnp.concatenate([v * b, kb * eg], axis=1))
        u = sol[:, :DV_B]
        w = sol[:, DV_B:]
        attn = jnp.where(tri, _mm_nt(q, k) * decay, 0.0)
        qe = q * eg
        kd = k * jnp.exp(g_last - gc)
        v_new, o_state = [], []
        for h in range(nh):
            hs = slice(h * c, (h + 1) * c)
            s = s_scr[h]
            ws = _mm(jnp.concatenate([w[hs], qe[hs]], axis=0), s)
            vn = u[hs] - ws[:c]
            v_new.append(vn)
            o_state.append(ws[c:])
            s_scr[h] = s * jnp.exp(g_last[hs][0:1, :]) + _mm_tn(kd[hs], vn)
        o = jnp.concatenate(o_state, axis=0) + _mm(attn, jnp.concatenate(v_new, axis=0))
        on = o * lax.rsqrt(jnp.mean(o * o, axis=-1, keepdims=True) + EPS) * ng_ref[...]
        zz = stack(z_ref, 0, DV_B)
        gated = (on * (zz * _sigmoid(zz))).astype(ob_ref.dtype)
        for h in range(nh):
            ob_ref[rs, h * DV_B:(h + 1) * DV_B] = gated[h * c:(h + 1) * c]

    @pl.when(t == pl.num_programs(1) - 1)
    def _():
        sout_ref[...] = s_scr[...]


def _gdn_call(xc, z, misc, conv0, s0, convw, alog_pad, dtb_pad, ng, c, cpt):
    b, t, _ = xc.shape
    tt = c * cpt
    tok = lambda w: pl.BlockSpec((None, tt, w), lambda i, j: (i, j, 0))
    per_b = lambda shp: pl.BlockSpec((None,) + shp, lambda i, j: (i,) + (0,) * len(shp))
    return pl.pallas_call(
        functools.partial(_gdn_kernel, c=c, cpt=cpt),
        out_shape=[jax.ShapeDtypeStruct((b, t, WIDTH_V_B), BF16),
                   jax.ShapeDtypeStruct((b, CONV_W - 1, CONV_CH), F32),
                   jax.ShapeDtypeStruct((b, N_HEADS_B, DK_B, DV_B), F32)],
        grid=(b, t // tt),
        in_specs=[tok(CONV_CH), tok(WIDTH_V_B), tok(MISC_W), per_b((CONV_W - 1, CONV_CH)),
                  per_b((N_HEADS_B, DK_B, DV_B)),
                  pl.BlockSpec((CONV_W, CONV_CH), lambda i, j: (0, 0)),
                  pl.BlockSpec((1, MISC_W), lambda i, j: (0, 0)),
                  pl.BlockSpec((1, MISC_W), lambda i, j: (0, 0)),
                  pl.BlockSpec((1, DV_B), lambda i, j: (0, 0))],
        out_specs=[tok(WIDTH_V_B), per_b((CONV_W - 1, CONV_CH)), per_b((N_HEADS_B, DK_B, DV_B))],
        scratch_shapes=[pltpu.VMEM((tt + 8, CONV_CH), F32), pltpu.VMEM((tt, CONV_CH), F32),
                        pltpu.VMEM((N_HEADS_B, DK_B, DV_B), F32)],
        compiler_params=_params(2),
        name="gated_deltanet",
    )(xc, z, misc, conv0, s0, convw, alog_pad, dtb_pad, ng)


def _score_key(score):
    bits = pltpu.bitcast(score, I32)
    mag = bits & INT_MAX
    return jnp.where(bits < 0, -mag, mag)


def _key_score(key):
    bits = jnp.where(key < 0, (-key) | INT_MIN, key)
    return pltpu.bitcast(bits, F32)


def _dsa_kernel(qa_ref, qi_ref, misc_ref, k_ref, v0_ref, v1_ref, ki_ref, o_ref,
                sc_scr, m_scr, acc_scr, alpha_scr, s_scr, *, tq, kt, past, n_keys, n_sel):
    q0 = pl.program_id(1) * tq
    row = lax.broadcasted_iota(I32, (tq, 1), 0)
    pos = past + q0 + row

    def key_limit(p):
        return jnp.minimum((lax.shift_right_logical(p, 6) + 1) * CHUNK, n_keys)

    limit = key_limit(pos)
    nkt = lax.div(key_limit(past + q0 + tq - 1) + kt - 1, kt)
    n_full = lax.div(key_limit(past + q0), kt)
    kkf = jnp.minimum(n_sel, limit).astype(F32)

    misc = misc_ref[...]
    qi = (qi_ref[...] * (IDX_DIM ** -0.5)).astype(BF16)
    qi_h = [qi[:, h * IDX_DIM:(h + 1) * IDX_DIM] for h in range(N_IDX_HEADS)]
    wis = [jnp.broadcast_to(misc[:, MISC_WI + h:MISC_WI + h + 1] * (N_IDX_HEADS ** -0.5), (tq, kt))
           for h in range(N_IDX_HEADS)]

    def score_tile(j, masked):
        ki = ki_ref[pl.ds(pl.multiple_of(j * kt, kt), kt), :]
        score = None
        for h in range(N_IDX_HEADS):
            term = jnp.maximum(_dot_nt(qi_h[h], ki), 0.0) * wis[h]
            score = term if score is None else score + term
        if masked:
            lane = lax.broadcasted_iota(I32, (tq, kt), 1)
            score = jnp.where(lane + j * kt < limit, score, -jnp.inf)
        sc_scr[j] = score

    def full_tile_pairs(p, carry):
        score_tile(2 * p, False)
        score_tile(2 * p + 1, False)
        return carry

    def edge_tiles(j, carry):
        score_tile(j, True)
        return carry

    n_pairs = lax.div(n_full, 2)
    lax.fori_loop(0, n_pairs, full_tile_pairs, 0)
    lax.fori_loop(2 * n_pairs, nkt, edge_tiles, 0)

    def top2(j, carry):
        m1, m2 = carry
        x = sc_scr[j]
        for s in range(kt // LANES):
            xs = x[:, s * LANES:(s + 1) * LANES]
            m2 = jnp.maximum(m2, jnp.minimum(m1, xs))
            m1 = jnp.maximum(m1, xs)
        return m1, m2

    neg_inf = jnp.full((tq, LANES), -jnp.inf, F32)
    m1, m2 = lax.fori_loop(0, nkt, top2, (neg_inf, neg_inf))
    row_max = jnp.max(m1, axis=1, keepdims=True)
    floor = jnp.min(m2, axis=1, keepdims=True) if n_sel <= 2 * LANES else jnp.full((tq, 1), -jnp.inf, F32)
    lo0 = _score_key(floor)
    hi0 = _score_key(row_max) + 1

    def count_ge(thr):
        def body(j, acc):
            c = jnp.where(sc_scr[j] >= thr, 1.0, 0.0)
            part = c[:, 0:LANES]
            for s in range(1, kt // LANES):
                part = part + c[:, s * LANES:(s + 1) * LANES]
            return acc + part
        acc = lax.fori_loop(0, nkt, body, jnp.zeros((tq, LANES), F32))
        return jnp.sum(acc, axis=1, keepdims=True)

    def n_open(lo, hi):
        return jnp.max(jnp.where(hi != lo + 1, 1.0, 0.0))

    def bis_cond(st):
        return st[3] > 0.5

    def bis_body(st):
        lo, hi, chi, _ = st
        lo, hi, chi = bis_step(lo, hi, chi)
        lo, hi, chi = bis_step(lo, hi, chi)
        return lo, hi, chi, n_open(lo, hi)

    def bis_step(lo, hi, chi):
        mid = (lo >> 1) + (hi >> 1) + (lo & hi & 1)
        mid = jnp.where(lo < 0, jnp.where(hi > 0, 0, mid), jnp.where(lo == 0, jnp.where(hi > 1, 1, mid), mid))
        c = count_ge(_key_score(mid))
        ge = c >= kkf
        exact = c == kkf
        lo_n = jnp.where(ge, mid, lo)
        hi_n = jnp.where(exact, mid + 1, jnp.where(ge, hi, mid))
        chi_n = jnp.where(exact, NEG_BIG, jnp.where(ge, chi, c))
        return lo_n, hi_n, chi_n

    tau_key, _, cnt_gt, _ = lax.while_loop(bis_cond, bis_body,
                                           (lo0, hi0, jnp.zeros((tq, 1), F32), n_open(lo0, hi0)))
    tau = _key_score(tau_key)
    need = kkf - cnt_gt

    ur = lax.broadcasted_iota(I32, (kt, kt), 0)
    uc = lax.broadcasted_iota(I32, (kt, kt), 1)
    before = (ur < uc).astype(BF16)
    m_scr[...] = jnp.full(m_scr.shape, NEG_BIG, F32)
    acc_scr[...] = jnp.zeros(acc_scr.shape, F32)
    v_refs = (v0_ref, v1_ref)

    def attn_tile(j, tie_seen):
        x = sc_scr[j]
        eq = x == tau
        eqf = jnp.where(eq, 1.0, 0.0)
        rank = tie_seen + _dot(eqf.astype(BF16), before)
        sel = (x > tau) | (eq & (rank < need))
        bias = jnp.where(sel, 0.0, NEG_BIG)
        start = pl.multiple_of(j * kt, kt)
        kk = k_ref[pl.ds(start, kt), :]
        for hq in range(N_HEADS_A):
            s = _dot_nt(qa_ref[:, hq * LANES:(hq + 1) * LANES], kk) + bias
            s_scr[hq] = s
            m_old = m_scr[hq]
            m_new = jnp.maximum(m_old, jnp.max(s, axis=1, keepdims=True))
            alpha_scr[hq] = jnp.exp2(m_old - m_new)
            m_scr[hq] = m_new
        for hq in range(N_HEADS_A):
            vv = v_refs[hq // GROUP_A][pl.ds(start, kt), :]
            p = jnp.exp2(s_scr[hq] - jnp.concatenate([m_scr[hq]] * (kt // LANES), axis=1))
            acc_scr[hq] = alpha_scr[hq] * acc_scr[hq] + _dot(p.astype(BF16), vv)
        return tie_seen + jnp.sum(eqf, axis=1, keepdims=True)

    lax.fori_loop(0, nkt, attn_tile, jnp.zeros((tq, 1), F32))

    for hq in range(N_HEADS_A):
        acc = acc_scr[hq]
        o_ref[:, hq * LANES:(hq + 1) * LANES] = (acc / acc[:, HEAD_DIM_A:HEAD_DIM_A + 1]).astype(o_ref.dtype)


def _dsa_call(qa, qi, misc, kb, v0, v1, kib, tq, kt, past, n_keys):
    b, t, _ = qa.shape
    lp = kb.shape[1]
    n_sel = min(TOPK_MAX, n_keys // 4)
    qtok = lambda w: pl.BlockSpec((None, tq, w), lambda i, j: (i, j, 0))
    keys = lambda w: pl.BlockSpec((None, lp, w), lambda i, j: (i, 0, 0))
    return pl.pallas_call(
        functools.partial(_dsa_kernel, tq=tq, kt=kt, past=past, n_keys=n_keys, n_sel=n_sel),
        out_shape=jax.ShapeDtypeStruct((b, t, WIDTH_A_PAD), BF16),
        grid=(b, t // tq),
        in_specs=[qtok(WIDTH_A_PAD), qtok(WIDTH_IDX_Q), qtok(MISC_W),
                  keys(WIDTH_KV_A), keys(LANES), keys(LANES), keys(IDX_DIM)],
        out_specs=qtok(WIDTH_A_PAD),
        scratch_shapes=[pltpu.VMEM((lp // kt, tq, kt), F32),
                        pltpu.VMEM((N_HEADS_A, tq, LANES), F32),
                        pltpu.VMEM((N_HEADS_A, tq, LANES), F32),
                        pltpu.VMEM((N_HEADS_A, tq, LANES), F32),
                        pltpu.VMEM((N_HEADS_A, tq, kt), F32)],
        compiler_params=_params(2),
        name="dsa",
    )(qa, qi, misc, kb, v0, v1, kib)


def _pick_tile(n, pref):
    t = min(n, pref)
    while n % t:
        t //= 2
    return t


def _pad_keys(a, lp):
    return jnp.pad(a, ((0, 0), (0, lp - a.shape[1]), (0, 0)))


def _value_with_ones(v_head):
    b, l, _ = v_head.shape
    tail = jnp.zeros((b, l, LANES - HEAD_DIM_A), v_head.dtype).at[:, :, 0].set(1.0)
    return jnp.concatenate([v_head, tail], axis=-1).astype(BF16)


def _layer(x, w, past, final_norm, final):
    b, t, d = x.shape
    n = b * t
    tm = _pick_tile(n, 256)
    tm_big = _pick_tile(n, 512)
    x_flat = x.reshape(n, d)
    x1 = _ffn_call(x_flat, w["ffn1_norm"], w["ffn1_w_gate"], w["ffn1_w_up"], w["ffn1_w_down"], tm_big, 2)
    (qa, k, v, kb, v0, v1, qi, ki, kib, misc, xc, z, ga, gb) = _proj_call(x1, w["mix_norm"], w["w_in_groups"], tm_big)
    per_batch = lambda a: a.reshape(b, t, a.shape[-1])

    if past is None:
        conv0 = jnp.zeros((b, CONV_W - 1, CONV_CH), F32)
        s0 = jnp.zeros((b, N_HEADS_B, DK_B, DV_B), F32)
        chunk, past_len = CHUNK, 0
        kb_all, v0_all, v1_all, kib_all = per_batch(kb), per_batch(v0), per_batch(v1), per_batch(kib)
    else:
        ck, cv, cki, conv0, s0 = past
        chunk, past_len = t, ck.shape[1]
        kb_all = jnp.concatenate([ck.reshape(b, past_len, WIDTH_KV_A).astype(BF16), per_batch(kb)], axis=1)
        v0_all = jnp.concatenate([_value_with_ones(cv[:, :, 0, :]), per_batch(v0)], axis=1)
        v1_all = jnp.concatenate([_value_with_ones(cv[:, :, 1, :]), per_batch(v1)], axis=1)
        kib_all = jnp.concatenate([cki.astype(BF16), per_batch(kib)], axis=1)
    n_keys = past_len + t
    kt = 512
    lp = -(-n_keys // kt) * kt
    if lp != n_keys:
        kb_all, v0_all, v1_all, kib_all = [_pad_keys(a, lp) for a in (kb_all, v0_all, v1_all, kib_all)]

    cpt = max(1, _pick_tile(t, 256) // chunk)
    ob, conv_new, s_new = _gdn_call(per_batch(xc), per_batch(z), per_batch(misc), conv0, s0, w["conv_w"],
                                    w["alog_pad"], w["dtb_pad"], w["delta_norm"], chunk, cpt)
    tq = _pick_tile(t, 256)
    oa = _dsa_call(per_batch(qa), per_batch(qi), per_batch(misc), kb_all, v0_all, v1_all, kib_all,
                   tq, kt, past_len, n_keys)

    y = _out_call(x1, oa.reshape(n, WIDTH_A_PAD), ob.reshape(n, WIDTH_V_B), ga, gb,
                  w["w_branch_a"], w["w_branch_b"], w["w_out"],
                  w["ffn2_norm"], w["ffn2_w_gate"], w["ffn2_w_up"], w["ffn2_w_down"], final_norm, final, tm, 2)
    return (y.reshape(b, t, d),
            (k.reshape(b, t, N_KV_A, HEAD_DIM_A), v.reshape(b, t, N_KV_A, HEAD_DIM_A), per_batch(ki),
             conv_new, s_new))


def _split_w_in(w_in):
    d = w_in.shape[0]
    sizes = (WIDTH_A, WIDTH_KV_A, WIDTH_KV_A, WIDTH_IDX_Q, IDX_DIM, N_IDX_HEADS,
             WIDTH_QK_B, WIDTH_QK_B, WIDTH_V_B, WIDTH_V_B, N_HEADS_B, N_HEADS_B, d, d)
    offs = [0]
    for s in sizes:
        offs.append(offs[-1] + s)
    col = lambda i: w_in[:, offs[i]:offs[i + 1]]
    qa, ka, va, qi, ki, wi, qb, kb_, vb, zb, bb, ab, gate_a, gate_b = [col(i) for i in range(len(sizes))]
    qa_pad = jnp.zeros((d, WIDTH_A_PAD), w_in.dtype)
    for hq in range(N_HEADS_A):
        dst = hq * LANES + (hq // GROUP_A) * HEAD_DIM_A
        qa_pad = qa_pad.at[:, dst:dst + HEAD_DIM_A].set(qa[:, hq * HEAD_DIM_A:(hq + 1) * HEAD_DIM_A])
    v_pads = [jnp.zeros((d, LANES), w_in.dtype).at[:, :HEAD_DIM_A].set(va[:, n * HEAD_DIM_A:(n + 1) * HEAD_DIM_A])
              for n in range(N_KV_A)]
    misc = jnp.zeros((d, MISC_W), w_in.dtype)
    misc = misc.at[:, 0:IDX_DIM].set(ki)
    misc = misc.at[:, MISC_WI:MISC_WI + N_IDX_HEADS].set(wi)
    misc = misc.at[:, MISC_BETA:MISC_BETA + N_HEADS_B].set(bb)
    misc = misc.at[:, MISC_A:MISC_A + N_HEADS_B].set(ab)
    xc = jnp.concatenate([qb, kb_, vb], axis=1)
    return [m.astype(BF16) for m in (qa_pad, ka, va, v_pads[0], v_pads[1], qi, misc, xc, zb, gate_a, gate_b)]


def _pad_branch_a(w_branch_a):
    d = w_branch_a.shape[1]
    out = jnp.zeros((WIDTH_A_PAD, d), w_branch_a.dtype)
    for hq in range(N_HEADS_A):
        out = out.at[hq * LANES:hq * LANES + HEAD_DIM_A].set(w_branch_a[hq * HEAD_DIM_A:(hq + 1) * HEAD_DIM_A])
    return out


def _lane_pad(vec, start):
    return jnp.zeros((1, MISC_W), F32).at[0, start:start + vec.shape[0]].set(vec)


def kernel(x_prompt, x_sample, cache_attn_k, cache_attn_v, cache_idx_k, state_conv, state_delta,
           ffn1_norm, ffn1_w_gate, ffn1_w_up, ffn1_w_down, mix_norm, w_in, conv_w, a_log, dt_bias,
           delta_norm, w_branch_a, w_branch_b, w_out, ffn2_norm, ffn2_w_gate, ffn2_w_up, ffn2_w_down,
           final_norm):
    depth = ffn1_norm.shape[0]
    xp, xs = x_prompt, x_sample
    st_p, st_s = [], []
    row = lambda a: a.reshape(1, -1)
    for l in range(depth):
        w = {
            "ffn1_norm": row(ffn1_norm[l]), "ffn1_w_gate": ffn1_w_gate[l].astype(BF16),
            "ffn1_w_up": ffn1_w_up[l].astype(BF16), "ffn1_w_down": ffn1_w_down[l].astype(BF16),
            "mix_norm": row(mix_norm[l]), "w_in_groups": _split_w_in(w_in[l]), "conv_w": conv_w[l],
            "alog_pad": _lane_pad(a_log[l], MISC_A), "dtb_pad": _lane_pad(dt_bias[l], MISC_A),
            "delta_norm": row(delta_norm[l]),
            "w_branch_a": _pad_branch_a(w_branch_a[l]).astype(BF16), "w_branch_b": w_branch_b[l].astype(BF16),
            "w_out": w_out[l].astype(BF16),
            "ffn2_norm": row(ffn2_norm[l]), "ffn2_w_gate": ffn2_w_gate[l].astype(BF16),
            "ffn2_w_up": ffn2_w_up[l].astype(BF16), "ffn2_w_down": ffn2_w_down[l].astype(BF16),
        }
        last = l == depth - 1
        fin = row(final_norm)
        xp, sp = _layer(xp, w, None, fin, last)
        xs, ss = _layer(xs, w, (cache_attn_k[l], cache_attn_v[l], cache_idx_k[l], state_conv[l], state_delta[l]),
                        fin, last)
        st_p.append(sp)
        st_s.append(ss)

    def stacked(states, i):
        return jnp.stack([s[i] for s in states], axis=0)

    return (xp, xs,
            stacked(st_p, 0), stacked(st_p, 1), stacked(st_p, 2), stacked(st_p, 3), stacked(st_p, 4),
            stacked(st_s, 0), stacked(st_s, 1), stacked(st_s, 2), stacked(st_s, 3), stacked(st_s, 4))
```
